```python
import math
import jax, jax.numpy as jnp
from jax import lax
import numpy as np

D_MODEL = 1024
BATCH = 8
SEQ = 2048
DEPTH = 2
DEC_BATCH = 1
DEC_SEQ = 16384
PAST_LEN = 128

CONV_DIM = 512
CONV_WIDTH = 3
ATTN_HEADS = 8
ATTN_KV_HEADS = 2
ATTN_HEAD_DIM = 64
ATTN_DIM = ATTN_HEADS * ATTN_HEAD_DIM
ATTN_KV_DIM = ATTN_KV_HEADS * ATTN_HEAD_DIM
WINDOW = 128
ATTN_BLOCK = 128
MLSTM_HEADS = 4
MLSTM_HEAD_DIM = 128
MLSTM_DIM = MLSTM_HEADS * MLSTM_HEAD_DIM
MLSTM_CHUNK = 128
FORGET_BIAS = 3.0
M_INIT = -1e30
D_FF = 4 * D_MODEL
RMS_EPS = 1e-6

_IN_SIZES = (CONV_DIM, CONV_DIM, CONV_DIM,
             ATTN_DIM, ATTN_KV_DIM, ATTN_KV_DIM,
             MLSTM_DIM, MLSTM_DIM, MLSTM_DIM, MLSTM_DIM,
             4 * MLSTM_HEADS,
             D_MODEL, D_MODEL, D_MODEL)
N_IN = sum(_IN_SIZES)

kernel_name = "hybrid_bidir_conv_swa_mlstm_encoder"


def _split_cols(u, sizes):
    idx, acc = [], 0
    for s in sizes[:-1]:
        acc += s
        idx.append(acc)
    return jnp.split(u, idx, axis=-1)


def _rmsnorm(x, g):
    xf = x.astype(jnp.float32)
    r = lax.rsqrt(jnp.mean(xf * xf, axis=-1, keepdims=True) + RMS_EPS)
    return (xf * r).astype(x.dtype) * g


def _short_conv(z, w):
    zp = jnp.pad(z, ((0, 0), (1, 1), (0, 0)))
    return w[0] * zp[:, :-2] + w[1] * zp[:, 1:-1] + w[2] * zp[:, 2:]


def _window_attention(q, k, v, sink):
    Bn, S = q.shape[0], q.shape[1]
    nb = S // ATTN_BLOCK
    G = ATTN_HEADS // ATTN_KV_HEADS
    qb = q.reshape(Bn, nb, ATTN_BLOCK, ATTN_KV_HEADS, G, ATTN_HEAD_DIM)
    pad = ((0, 0), (ATTN_BLOCK, ATTN_BLOCK), (0, 0), (0, 0))
    kp = jnp.pad(k, pad).reshape(Bn, nb + 2, ATTN_BLOCK, ATTN_KV_HEADS, ATTN_HEAD_DIM)
    vp = jnp.pad(v, pad).reshape(Bn, nb + 2, ATTN_BLOCK, ATTN_KV_HEADS, ATTN_HEAD_DIM)
    kw = jnp.concatenate([kp[:, :-2], kp[:, 1:-1], kp[:, 2:]], axis=2)
    vw = jnp.concatenate([vp[:, :-2], vp[:, 1:-1], vp[:, 2:]], axis=2)
    scale = ATTN_HEAD_DIM ** -0.5
    s = jnp.einsum('bnqhgd,bnkhd->bnhgqk', qb, kw).astype(jnp.float32) * scale
    rel = (jnp.arange(3 * ATTN_BLOCK)[None, :] - ATTN_BLOCK
           - jnp.arange(ATTN_BLOCK)[:, None])
    keypos = (jnp.arange(nb)[:, None] * ATTN_BLOCK - ATTN_BLOCK
              + jnp.arange(3 * ATTN_BLOCK)[None, :])
    in_range = (keypos >= 0) & (keypos < S)
    valid = (jnp.abs(rel) <= WINDOW)[None, :, :] & in_range[:, None, :]
    slopes = jnp.exp2(-8.0 * (jnp.arange(ATTN_HEADS, dtype=jnp.float32) + 1.0) / ATTN_HEADS)
    slopes = slopes.reshape(ATTN_KV_HEADS, G)
    s = s - slopes[:, :, None, None] * jnp.abs(rel).astype(jnp.float32)
    s = jnp.where(valid[None, :, None, None], s, -jnp.inf)
    sk = sink.astype(jnp.float32).reshape(ATTN_KV_HEADS, G)[:, :, None, None]
    m = jnp.maximum(s.max(axis=-1, keepdims=True), sk)
    p = jnp.exp(s - m)
    denom = p.sum(axis=-1, keepdims=True) + jnp.exp(sk - m)
    out = jnp.einsum('bnhgqk,bnkhd->bnqhgd', (p / denom).astype(v.dtype), vw)
    return out.reshape(Bn, S, ATTN_HEADS * ATTN_HEAD_DIM)


def _mlstm_forward_dir(q, k, v, ig, fg):
    Bn, S, NH, DH = q.shape
    L = MLSTM_CHUNK
    nc = S // L
    qc = q.reshape(Bn, nc, L, NH, DH)
    kc = k.reshape(Bn, nc, L, NH, DH) * (DH ** -0.5)
    vc = v.reshape(Bn, nc, L, NH, DH)
    logf = jnp.swapaxes(jax.nn.log_sigmoid(fg).reshape(Bn, nc, L, NH), -1, -2)
    ii = jnp.swapaxes(ig.reshape(Bn, nc, L, NH), -1, -2)
    b = jnp.cumsum(logf, axis=-1)
    g = b[..., -1]
    w_state = g[..., None] - b + ii
    m_loc = w_state.max(axis=-1)
    e_state = jnp.swapaxes(jnp.exp(w_state - m_loc[..., None]), -1, -2)
    ke = kc * e_state[..., None]
    C_loc = jnp.einsum('bcshv,bcshk->bchvk', vc, ke)
    n_loc = ke.sum(axis=2)

    def step(carry, xs):
        C, n, m = carry
        g_c, m_loc_c, C_loc_c, n_loc_c = xs
        m_new = jnp.maximum(g_c + m, m_loc_c)
        a = jnp.exp(g_c + m - m_new)
        c = jnp.exp(m_loc_c - m_new)
        C_new = a[..., None, None] * C + c[..., None, None] * C_loc_c
        n_new = a[..., None] * n + c[..., None] * n_loc_c
        return (C_new, n_new, m_new), (C, n, m)

    init = (jnp.zeros((Bn, NH, DH, DH), jnp.float32),
            jnp.zeros((Bn, NH, DH), jnp.float32),
            jnp.full((Bn, NH), M_INIT, jnp.float32))
    xs = (jnp.moveaxis(g, 1, 0), jnp.moveaxis(m_loc, 1, 0),
          jnp.moveaxis(C_loc, 1, 0), jnp.moveaxis(n_loc, 1, 0))
    _, (C_prev, n_prev, m_prev) = lax.scan(step, init, xs)
    C_prev = jnp.moveaxis(C_prev, 0, 1)
    n_prev = jnp.moveaxis(n_prev, 0, 1)
    m_prev = jnp.moveaxis(m_prev, 0, 1)

    causal = jnp.tril(jnp.ones((L, L), dtype=bool))
    D = jnp.where(causal, b[..., :, None] - b[..., None, :] + ii[..., None, :], -jnp.inf)
    m_inter = b + m_prev[..., None]
    m_t = jnp.maximum(m_inter, D.max(axis=-1))
    P = jnp.exp(D - m_t[..., None]) * jnp.einsum('bcqhd,bckhd->bchqk', qc, kc)
    a_t = jnp.swapaxes(jnp.exp(m_inter - m_t), -1, -2)
    num = (jnp.einsum('bchqk,bckhd->bcqhd', P, vc)
           + a_t[..., None] * jnp.einsum('bchvk,bcqhk->bcqhv', C_prev, qc))
    den = (jnp.swapaxes(P.sum(axis=-1), -1, -2)
           + a_t * jnp.einsum('bchk,bcqhk->bcqh', n_prev, qc))
    lower = jnp.exp(-jnp.swapaxes(m_t, -1, -2))
    h = num / jnp.maximum(jnp.abs(den), lower)[..., None]
    return h.reshape(Bn, S, NH, DH)


def _layer(x, w_in, conv_w, attn_sink, gate_b, mnorm_g, w_a, w_b, w_c, w_o,
           g_mix, g_mlp, w_up, w_down):
    Bn, S, _ = x.shape
    h = _rmsnorm(x, g_mix)
    u = h @ w_in
    (cb, cc, cx, aq, ak, av, mq, mk, mv, mo, gif, ga, gb, gc) = _split_cols(u, _IN_SIZES)

    y_a = (cb * _short_conv(cc * cx, conv_w)) @ w_a

    att = _window_attention(aq.reshape(Bn, S, ATTN_HEADS, ATTN_HEAD_DIM),
                            ak.reshape(Bn, S, ATTN_KV_HEADS, ATTN_HEAD_DIM),
                            av.reshape(Bn, S, ATTN_KV_HEADS, ATTN_HEAD_DIM), attn_sink)
    y_b = att @ w_b

    gates = (gif + gate_b).astype(jnp.float32)
    i_f, f_f, i_b, f_b = jnp.split(gates, 4, axis=-1)
    shp = (Bn, S, MLSTM_HEADS, MLSTM_HEAD_DIM)
    q = mq.reshape(shp).astype(jnp.float32)
    k = mk.reshape(shp).astype(jnp.float32)
    v = mv.reshape(shp).astype(jnp.float32)
    h_fwd = _mlstm_forward_dir(q, k, v, i_f, f_f)
    h_bwd = jnp.flip(_mlstm_forward_dir(jnp.flip(q, 1), jnp.flip(k, 1), jnp.flip(v, 1),
                                        jnp.flip(i_b, 1), jnp.flip(f_b, 1)), axis=1)
    hm = h_fwd + h_bwd
    hm = hm * lax.rsqrt(jnp.mean(hm * hm, axis=-1, keepdims=True) + RMS_EPS)
    hm = hm.reshape(Bn, S, MLSTM_DIM).astype(x.dtype) * mnorm_g
    y_c = (jax.nn.sigmoid(mo) * hm) @ w_c

    merged = jax.nn.sigmoid(ga) * y_a + jax.nn.sigmoid(gb) * y_b + jax.nn.sigmoid(gc) * y_c
    x = x + merged @ w_o

    h2 = _rmsnorm(x, g_mlp)
    x = x + jnp.square(jax.nn.relu(h2 @ w_up)) @ w_down
    return x


def _trunk(x, w_in, conv_w, attn_sink, mlstm_gate_b, mlstm_norm_g, w_out_a, w_out_b,
           w_out_c, w_o, norm_mix_g, norm_mlp_g, w_mlp_up, w_mlp_down, norm_final_g):
    for l in range(DEPTH):
        x = _layer(x, w_in[l], conv_w[l], attn_sink[l], mlstm_gate_b[l], mlstm_norm_g[l],
                   w_out_a[l], w_out_b[l], w_out_c[l], w_o[l], norm_mix_g[l], norm_mlp_g[l],
                   w_mlp_up[l], w_mlp_down[l])
    return _rmsnorm(x, norm_final_g)


def setup_inputs(seed: int = 0) -> dict:
    key = jax.random.key(seed)
    ks = jax.random.split(key, 20)

    def nrm(k, shape, scale):
        return jax.random.normal(k, shape, jnp.float32) * scale

    gate_offset = jnp.array([0.0, FORGET_BIAS, 0.0, FORGET_BIAS], jnp.float32)[None, :, None]
    mlstm_gate_b = (nrm(ks[5], (DEPTH, 4, MLSTM_HEADS), 0.1) + gate_offset).reshape(DEPTH, 4 * MLSTM_HEADS)
    return {
        "x_prompt": nrm(ks[0], (BATCH, SEQ, D_MODEL), 1.0),
        "x_sample": nrm(ks[1], (DEC_BATCH, DEC_SEQ, D_MODEL), 1.0),
        "w_in": nrm(ks[2], (DEPTH, D_MODEL, N_IN), D_MODEL ** -0.5),
        "conv_w": nrm(ks[3], (DEPTH, CONV_WIDTH, CONV_DIM), CONV_WIDTH ** -0.5),
        "attn_sink": nrm(ks[4], (DEPTH, ATTN_HEADS), 0.5),
        "mlstm_gate_b": mlstm_gate_b,
        "mlstm_norm_g": 1.0 + nrm(ks[6], (DEPTH, MLSTM_DIM), 0.02),
        "w_out_a": nrm(ks[7], (DEPTH, CONV_DIM, D_MODEL), CONV_DIM ** -0.5),
        "w_out_b": nrm(ks[8], (DEPTH, ATTN_DIM, D_MODEL), ATTN_DIM ** -0.5),
        "w_out_c": nrm(ks[9], (DEPTH, MLSTM_DIM, D_MODEL), MLSTM_DIM ** -0.5),
        "w_o": nrm(ks[10], (DEPTH, D_MODEL, D_MODEL), D_MODEL ** -0.5),
        "norm_mix_g": 1.0 + nrm(ks[11], (DEPTH, D_MODEL), 0.02),
        "norm_mlp_g": 1.0 + nrm(ks[12], (DEPTH, D_MODEL), 0.02),
        "w_mlp_up": nrm(ks[13], (DEPTH, D_MODEL, D_FF), D_MODEL ** -0.5),
        "w_mlp_down": nrm(ks[14], (DEPTH, D_FF, D_MODEL), D_FF ** -0.5),
        "norm_final_g": 1.0 + nrm(ks[15], (D_MODEL,), 0.02),
    }


def reference(x_prompt, x_sample, w_in, conv_w, attn_sink, mlstm_gate_b, mlstm_norm_g,
              w_out_a, w_out_b, w_out_c, w_o, norm_mix_g, norm_mlp_g, w_mlp_up, w_mlp_down,
              norm_final_g):
    y_prompt = _trunk(x_prompt, w_in, conv_w, attn_sink, mlstm_gate_b, mlstm_norm_g, w_out_a,
                      w_out_b, w_out_c, w_o, norm_mix_g, norm_mlp_g, w_mlp_up, w_mlp_down,
                      norm_final_g)
    y_sample = _trunk(x_sample, w_in, conv_w, attn_sink, mlstm_gate_b, mlstm_norm_g, w_out_a,
                      w_out_b, w_out_c, w_o, norm_mix_g, norm_mlp_g, w_mlp_up, w_mlp_down,
                      norm_final_g)
    return (y_prompt, y_sample)
```

```python
import functools

import jax
import jax.numpy as jnp
from jax import lax
from jax.experimental import pallas as pl
from jax.experimental.pallas import tpu as pltpu

F32 = jnp.float32
BF16 = jnp.bfloat16

LANES = 128
SUBLANES = 8
VMEM_LIMIT_BYTES = 56 * 1024 * 1024

D_MODEL = 1024
CONV_DIM = 512
ATTN_HEADS = 8
ATTN_KV_HEADS = 2
ATTN_GROUP = ATTN_HEADS // ATTN_KV_HEADS
ATTN_HEAD_DIM = 64
ATTN_DIM = ATTN_HEADS * ATTN_HEAD_DIM
ATTN_KV_DIM = ATTN_KV_HEADS * ATTN_HEAD_DIM
WINDOW = 128
ATTN_BLOCK = 128
MLSTM_HEADS = 4
MLSTM_HEAD_DIM = 128
MLSTM_DIM = MLSTM_HEADS * MLSTM_HEAD_DIM
MLSTM_CHUNK = 128
M_INIT = -1e30
D_FF = 4 * D_MODEL
RMS_EPS = 1e-6
N_GATES = 4 * MLSTM_HEADS

C_CONV = 0
C_AQ = C_CONV + 3 * CONV_DIM
C_AK = C_AQ + ATTN_DIM
C_AV = C_AK + ATTN_KV_DIM
C_ML = C_AV + ATTN_KV_DIM
C_MERGE = C_ML + 4 * MLSTM_DIM
C_GATE = C_MERGE + 3 * D_MODEL
N_PROJ = C_GATE + LANES

TILE = 512
FF_CHUNK = 1024


def _dot(a, b):
    return jnp.dot(a, b, preferred_element_type=F32)


def _dot_nt(a, b):
    return lax.dot_general(a, b, (((1,), (1,)), ((), ())), preferred_element_type=F32)


def _dot_tn(a, b):
    return lax.dot_general(a, b, (((0,), (0,)), ((), ())), preferred_element_type=F32)


def _rmsnorm(x, g):
    r = lax.rsqrt(jnp.mean(x * x, axis=-1, keepdims=True) + RMS_EPS)
    return (x * r) * g


def _log_sigmoid(x):
    return jnp.minimum(x, 0.0) - jnp.log(1.0 + jnp.exp(-jnp.abs(x)))


def _mlstm_chunk(q_all, k_all, v_all, ii, fg, c_ref, n_ref, m_ref, reverse):
    L = MLSTM_CHUNK
    scale = MLSTM_HEAD_DIM ** -0.5
    row = lax.broadcasted_iota(jnp.int32, (L, L), 0)
    col = lax.broadcasted_iota(jnp.int32, (L, L), 1)
    keep = (col >= row) if reverse else (col <= row)
    tri = jnp.where(keep, 1.0, 0.0).astype(F32)
    logf = _log_sigmoid(fg)
    b = jnp.dot(tri, logf, precision=lax.Precision.HIGHEST, preferred_element_type=F32)
    g = b[0:1, :] if reverse else b[L - 1:L, :]
    w_state = g - b + ii
    m_loc = jnp.max(w_state, axis=0, keepdims=True)
    e_state = jnp.exp(w_state - m_loc)
    m_prev = m_ref[...]
    m_inter = b + m_prev
    m_new = jnp.maximum(g + m_prev, m_loc)
    a_vec = jnp.exp(g + m_prev - m_new)
    c_vec = jnp.exp(m_loc - m_new)
    r_rows = jnp.transpose(ii - b)
    outs = []
    for h in range(MLSTM_HEADS):
        sl = slice(h * MLSTM_HEAD_DIM, (h + 1) * MLSTM_HEAD_DIM)
        qh, kh, vh = q_all[:, sl], k_all[:, sl], v_all[:, sl]
        b_h = b[:, h:h + 1]
        d = jnp.where(keep, b_h + r_rows[h:h + 1, :], -jnp.inf)
        mi_h = m_inter[:, h:h + 1]
        m_t = jnp.maximum(mi_h, jnp.max(d, axis=1, keepdims=True))
        s = _dot_nt(qh, kh) * scale
        p = jnp.exp(d - m_t) * s
        a_t = jnp.exp(mi_h - m_t)
        c_prev = c_ref[h]
        n_prev = n_ref[h:h + 1, :]
        inter = _dot_nt(qh, c_prev.astype(BF16))
        num = _dot(p.astype(BF16), vh) + a_t * inter
        nq = jnp.sum(qh.astype(F32) * n_prev, axis=1, keepdims=True)
        den = jnp.sum(p, axis=1, keepdims=True) + a_t * nq
        outs.append(num / jnp.maximum(jnp.abs(den), jnp.exp(-m_t)))
        ke = kh.astype(F32) * (scale * e_state[:, h:h + 1])
        c_loc = _dot_tn(vh, ke.astype(BF16))
        n_loc = jnp.sum(ke, axis=0, keepdims=True)
        a_h = a_vec[:, h:h + 1]
        c_h = c_vec[:, h:h + 1]
        c_ref[h] = a_h * c_prev + c_h * c_loc
        n_ref[h:h + 1, :] = a_h * n_prev + c_h * n_loc
    m_ref[...] = m_new
    return outs


def _init_state(c_ref, n_ref, m_ref):
    c_ref[...] = jnp.zeros(c_ref.shape, F32)
    n_ref[...] = jnp.zeros(n_ref.shape, F32)
    m_ref[...] = jnp.full(m_ref.shape, M_INIT, F32)


def _mix_in_kernel(x_ref, xp_ref, xn_ref, gmix_ref, w_ref, convw_ref, gateb_ref, wa_ref,
                   pa_ref, gbc_ref, aq_ref, kd_ref, vd_ref, mqkv_ref, smo_ref, gcol_ref, hf_ref,
                   c_scr, n_scr, m_scr, *, tile):
    i = pl.program_id(1)
    n_tiles = pl.num_programs(1)

    @pl.when(i == 0)
    def _():
        _init_state(c_scr, n_scr, m_scr)

    g_mix = gmix_ref[...]
    h = _rmsnorm(x_ref[...], g_mix).astype(BF16)

    halo = jnp.concatenate([_rmsnorm(xp_ref[...], g_mix), _rmsnorm(xn_ref[...], g_mix)], axis=0)
    uh = _dot(halo.astype(BF16), w_ref[:, C_CONV + CONV_DIM:C_CONV + 3 * CONV_DIM])
    zh = uh[:, :CONV_DIM] * uh[:, CONV_DIM:]
    z_prev = jnp.where(i > 0, zh[SUBLANES - 1:SUBLANES, :], 0.0)
    z_next = jnp.where(i < n_tiles - 1, zh[SUBLANES:SUBLANES + 1, :], 0.0)
    uc = _dot(h, w_ref[:, C_CONV:C_CONV + 3 * CONV_DIM])
    z = uc[:, CONV_DIM:2 * CONV_DIM] * uc[:, 2 * CONV_DIM:]
    trow = lax.broadcasted_iota(jnp.int32, (tile, CONV_DIM), 0)
    z_dn = jnp.where(trow == 0, z_prev, pltpu.roll(z, 1, axis=0))
    z_up = jnp.where(trow == tile - 1, z_next, pltpu.roll(z, tile - 1, axis=0))
    conv = convw_ref[0:1, :] * z_dn + convw_ref[1:2, :] * z + convw_ref[2:3, :] * z_up
    y_a = _dot((uc[:, :CONV_DIM] * conv).astype(BF16), wa_ref[...])

    um = _dot(h, w_ref[:, C_MERGE:C_MERGE + 3 * D_MODEL])
    pa_ref[...] = (jax.nn.sigmoid(um[:, :D_MODEL]) * y_a).astype(BF16)
    gbc_ref[...] = jax.nn.sigmoid(um[:, D_MODEL:]).astype(BF16)

    ua = _dot(h, w_ref[:, C_AQ:C_AV + ATTN_KV_DIM])
    aq_ref[...] = (ua[:, :ATTN_DIM] * (ATTN_HEAD_DIM ** -0.5)).astype(BF16)
    lane = lax.broadcasted_iota(jnp.int32, (tile, LANES), 1)
    low = lane < ATTN_HEAD_DIM
    for src, dst in ((ua[:, ATTN_DIM:ATTN_DIM + ATTN_KV_DIM], kd_ref),
                     (ua[:, ATTN_DIM + ATTN_KV_DIM:], vd_ref)):
        swapped = pltpu.roll(src, ATTN_HEAD_DIM, axis=1)
        dst[:, :LANES] = jnp.where(low, src, swapped).astype(BF16)
        dst[:, LANES:] = jnp.where(low, swapped, src).astype(BF16)

    ul = _dot(h, w_ref[:, C_ML:C_ML + 4 * MLSTM_DIM])
    mqkv_ref[...] = ul[:, :3 * MLSTM_DIM].astype(BF16)
    smo_ref[...] = jax.nn.sigmoid(ul[:, 3 * MLSTM_DIM:]).astype(BF16)
    gcol_ref[...] = _dot(h, w_ref[:, C_GATE:C_GATE + LANES]) + gateb_ref[...]

    def chunk(c, carry):
        r0 = pl.multiple_of(c * MLSTM_CHUNK, MLSTM_CHUNK)
        rows = pl.ds(r0, MLSTM_CHUNK)
        gates = gcol_ref[rows, :]
        ii = gates
        fg = pltpu.roll(gates, LANES - MLSTM_HEADS, axis=1)
        outs = _mlstm_chunk(mqkv_ref[rows, 0:MLSTM_DIM], mqkv_ref[rows, MLSTM_DIM:2 * MLSTM_DIM],
                            mqkv_ref[rows, 2 * MLSTM_DIM:3 * MLSTM_DIM], ii, fg,
                            c_scr, n_scr, m_scr, reverse=False)
        for hd, o in enumerate(outs):
            hf_ref[rows, hd * MLSTM_HEAD_DIM:(hd + 1) * MLSTM_HEAD_DIM] = o
        return carry

    lax.fori_loop(0, tile // MLSTM_CHUNK, chunk, 0)


def _mix_out_kernel(sink_ref, x_ref, pa_ref, gbc_ref, aq_ref, kd_ref, kdp_ref, kdn_ref,
                    vd_ref, vdp_ref, vdn_ref, mqkv_ref, smo_ref, gcol_ref, hf_ref, bias_ref,
                    mnorm_ref, wb_ref, wc_ref, wo_ref,
                    xo_ref,
                    kwin, vwin, att_scr, yc_scr, c_scr, n_scr, m_scr, *, tile):
    i = pl.program_id(1)
    n_tiles = pl.num_programs(1)
    ti = n_tiles - 1 - i
    n_qblk = tile // ATTN_BLOCK

    @pl.when(i == 0)
    def _():
        _init_state(c_scr, n_scr, m_scr)

    kwin[0:ATTN_BLOCK, :] = kdp_ref[...]
    kwin[ATTN_BLOCK:ATTN_BLOCK + tile, :] = kd_ref[...]
    kwin[ATTN_BLOCK + tile:, :] = kdn_ref[...]
    vwin[0:ATTN_BLOCK, :] = vdp_ref[...]
    vwin[ATTN_BLOCK:ATTN_BLOCK + tile, :] = vd_ref[...]
    vwin[ATTN_BLOCK + tile:, :] = vdn_ref[...]

    lane = lax.broadcasted_iota(jnp.int32, (ATTN_BLOCK, LANES), 1)
    low = lane < ATTN_HEAD_DIM
    kcol = lax.broadcasted_iota(jnp.int32, (1, 3 * ATTN_BLOCK), 1)

    def qblock(j, carry):
        r0 = pl.multiple_of(j * ATTN_BLOCK, ATTN_BLOCK)
        rows = pl.ds(r0, ATTN_BLOCK)
        wrows = pl.ds(r0, 3 * ATTN_BLOCK)
        before = jnp.logical_and(ti == 0, j == 0)
        after = jnp.logical_and(ti == n_tiles - 1, j == n_qblk - 1)
        outside = jnp.logical_or(jnp.logical_and(kcol < ATTN_BLOCK, before),
                                 jnp.logical_and(kcol >= 2 * ATTN_BLOCK, after))
        pen = jnp.where(outside, -jnp.inf, 0.0)
        for g in range(ATTN_KV_HEADS):
            kw = kwin[wrows, g * LANES:(g + 1) * LANES]
            vw = vwin[wrows, g * LANES:(g + 1) * LANES]
            qs = []
            for pr in range(2):
                pair = g * 2 + pr
                qp = aq_ref[rows, pair * LANES:(pair + 1) * LANES]
                qs.append(jnp.where(low, qp, jnp.zeros_like(qp)))
                qs.append(jnp.where(low, jnp.zeros_like(qp), qp))
            s_all = _dot_nt(jnp.concatenate(qs, axis=0), kw) + (bias_ref[g] + pen)
            outs = []
            for idx in range(ATTN_GROUP):
                sk = sink_ref[g * ATTN_GROUP + idx]
                s = s_all[idx * ATTN_BLOCK:(idx + 1) * ATTN_BLOCK, :]
                m = jnp.maximum(jnp.max(s, axis=1, keepdims=True), sk)
                p = jnp.exp(s - m)
                denom = jnp.sum(p, axis=1, keepdims=True) + jnp.exp(sk - m)
                outs.append(_dot(p.astype(BF16), vw) / denom)
            for pr in range(2):
                pair = g * 2 + pr
                att_scr[rows, pair * LANES:(pair + 1) * LANES] = jnp.where(
                    low, outs[2 * pr], outs[2 * pr + 1]).astype(BF16)
        return carry

    lax.fori_loop(0, n_qblk, qblock, 0)

    n_chunks = tile // MLSTM_CHUNK

    def chunk(cc, carry):
        c = n_chunks - 1 - cc
        r0 = pl.multiple_of(c * MLSTM_CHUNK, MLSTM_CHUNK)
        rows = pl.ds(r0, MLSTM_CHUNK)
        gates = gcol_ref[rows, :]
        ii = pltpu.roll(gates, LANES - 2 * MLSTM_HEADS, axis=1)
        fg = pltpu.roll(gates, LANES - 3 * MLSTM_HEADS, axis=1)
        outs = _mlstm_chunk(mqkv_ref[rows, 0:MLSTM_DIM], mqkv_ref[rows, MLSTM_DIM:2 * MLSTM_DIM],
                            mqkv_ref[rows, 2 * MLSTM_DIM:3 * MLSTM_DIM], ii, fg,
                            c_scr, n_scr, m_scr, reverse=True)
        for hd, o in enumerate(outs):
            sl = slice(hd * MLSTM_HEAD_DIM, (hd + 1) * MLSTM_HEAD_DIM)
            hm = hf_ref[rows, sl] + o
            hm = hm * lax.rsqrt(jnp.mean(hm * hm, axis=-1, keepdims=True) + RMS_EPS)
            yc_scr[rows, sl] = (smo_ref[rows, sl].astype(F32) * (hm * mnorm_ref[:, sl])).astype(BF16)
        return carry

    lax.fori_loop(0, n_chunks, chunk, 0)

    y_b = _dot(att_scr[...], wb_ref[...])
    y_c = _dot(yc_scr[...], wc_ref[...])
    merged = (pa_ref[...].astype(F32) + gbc_ref[:, :D_MODEL].astype(F32) * y_b
              + gbc_ref[:, D_MODEL:].astype(F32) * y_c)
    xo_ref[...] = x_ref[...] + _dot(merged.astype(BF16), wo_ref[...])


def _mlp_kernel(x_ref, g_ref, gf_ref, wup_ref, wdn_ref, o_ref, *, final_norm):
    x = x_ref[...]
    h = _rmsnorm(x, g_ref[...]).astype(BF16)
    acc = x
    for c in range(D_FF // FF_CHUNK):
        cols = slice(c * FF_CHUNK, (c + 1) * FF_CHUNK)
        up = jnp.maximum(_dot(h, wup_ref[:, cols]), 0.0)
        acc = acc + _dot((up * up).astype(BF16), wdn_ref[cols, :])
    if final_norm:
        acc = _rmsnorm(acc, gf_ref[...])
    o_ref[...] = acc


def _resident(shape):
    nd = len(shape)
    return pl.BlockSpec(shape, lambda b, i: (0,) * nd, pipeline_mode=pl.Buffered(1))


def _params():
    return pltpu.CompilerParams(dimension_semantics=("arbitrary", "arbitrary"),
                                vmem_limit_bytes=VMEM_LIMIT_BYTES)


def _mix_in(x, lw):
    bsz, seq, _ = x.shape
    tile = TILE
    n_tiles = seq // tile
    halo_per_tile = tile // SUBLANES
    n_halo = seq // SUBLANES

    def tok(width):
        return pl.BlockSpec((None, tile, width), lambda b, i: (b, i, 0))

    in_specs = [
        tok(D_MODEL),
        pl.BlockSpec((None, SUBLANES, D_MODEL),
                     lambda b, i: (b, jnp.maximum(i * halo_per_tile - 1, 0), 0)),
        pl.BlockSpec((None, SUBLANES, D_MODEL),
                     lambda b, i: (b, jnp.minimum((i + 1) * halo_per_tile, n_halo - 1), 0)),
        _resident((1, D_MODEL)),
        _resident((D_MODEL, N_PROJ)),
        _resident((3, CONV_DIM)),
        _resident((1, LANES)),
        _resident((CONV_DIM, D_MODEL)),
    ]
    widths = (D_MODEL, 2 * D_MODEL, ATTN_DIM, 2 * LANES, 2 * LANES, 3 * MLSTM_DIM, MLSTM_DIM)
    out_shape = [jax.ShapeDtypeStruct((bsz, seq, w), BF16) for w in widths]
    out_shape += [jax.ShapeDtypeStruct((bsz, seq, LANES), F32),
                  jax.ShapeDtypeStruct((bsz, seq, MLSTM_DIM), F32)]
    out_specs = [tok(w) for w in widths] + [tok(LANES), tok(MLSTM_DIM)]
    return pl.pallas_call(
        functools.partial(_mix_in_kernel, tile=tile),
        grid=(bsz, n_tiles),
        in_specs=in_specs,
        out_specs=out_specs,
        out_shape=out_shape,
        scratch_shapes=[pltpu.VMEM((MLSTM_HEADS, MLSTM_HEAD_DIM, MLSTM_HEAD_DIM), F32),
                        pltpu.VMEM((SUBLANES, MLSTM_HEAD_DIM), F32),
                        pltpu.VMEM((1, LANES), F32)],
        compiler_params=_params(),
        name="mix_in",
    )(x, x, x, lw["g_mix"], lw["w_proj"], lw["conv_w"], lw["gate_b"], lw["w_a"])


def _mix_out(x, mid, lw, attn_bias):
    pa, gbc, aq, kd, vd, mqkv, smo, gcol, hf = mid
    bsz, seq, _ = x.shape
    tile = TILE
    n_tiles = seq // tile
    blk_per_tile = tile // ATTN_BLOCK
    n_blk = seq // ATTN_BLOCK

    def tok(width):
        return pl.BlockSpec((None, tile, width), lambda b, i: (b, n_tiles - 1 - i, 0))

    def blk_prev(width):
        return pl.BlockSpec(
            (None, ATTN_BLOCK, width),
            lambda b, i: (b, jnp.maximum((n_tiles - 1 - i) * blk_per_tile - 1, 0), 0))

    def blk_next(width):
        return pl.BlockSpec(
            (None, ATTN_BLOCK, width),
            lambda b, i: (b, jnp.minimum((n_tiles - i) * blk_per_tile, n_blk - 1), 0))

    in_specs = [
        pl.BlockSpec(memory_space=pltpu.SMEM),
        tok(D_MODEL), tok(D_MODEL), tok(2 * D_MODEL), tok(ATTN_DIM),
        tok(2 * LANES), blk_prev(2 * LANES), blk_next(2 * LANES),
        tok(2 * LANES), blk_prev(2 * LANES), blk_next(2 * LANES),
        tok(3 * MLSTM_DIM), tok(MLSTM_DIM), tok(LANES), tok(MLSTM_DIM),
        _resident((ATTN_KV_HEADS, ATTN_GROUP * ATTN_BLOCK, 3 * ATTN_BLOCK)),
        _resident((1, MLSTM_DIM)),
        _resident((ATTN_DIM, D_MODEL)),
        _resident((MLSTM_DIM, D_MODEL)),
        _resident((D_MODEL, D_MODEL)),
    ]
    return pl.pallas_call(
        functools.partial(_mix_out_kernel, tile=tile),
        grid=(bsz, n_tiles),
        in_specs=in_specs,
        out_specs=tok(D_MODEL),
        out_shape=jax.ShapeDtypeStruct((bsz, seq, D_MODEL), F32),
        scratch_shapes=[pltpu.VMEM((tile + 2 * ATTN_BLOCK, 2 * LANES), BF16),
                        pltpu.VMEM((tile + 2 * ATTN_BLOCK, 2 * LANES), BF16),
                        pltpu.VMEM((tile, ATTN_DIM), BF16),
                        pltpu.VMEM((tile, MLSTM_DIM), BF16),
                        pltpu.VMEM((MLSTM_HEADS, MLSTM_HEAD_DIM, MLSTM_HEAD_DIM), F32),
                        pltpu.VMEM((SUBLANES, MLSTM_HEAD_DIM), F32),
                        pltpu.VMEM((1, LANES), F32)],
        compiler_params=_params(),
        name="mix_out",
    )(lw["sink"], x, pa, gbc, aq, kd, kd, kd, vd, vd, vd, mqkv, smo, gcol, hf, attn_bias,
      lw["mnorm_g"], lw["w_b"], lw["w_c"], lw["w_o"])


def _mlp(x, lw, g_final, final_norm):
    bsz, seq, _ = x.shape
    tile = TILE
    tok = pl.BlockSpec((None, tile, D_MODEL), lambda b, i: (b, i, 0))
    return pl.pallas_call(
        functools.partial(_mlp_kernel, final_norm=final_norm),
        grid=(bsz, seq // tile),
        in_specs=[tok, _resident((1, D_MODEL)), _resident((1, D_MODEL)),
                  _resident((D_MODEL, D_FF)), _resident((D_FF, D_MODEL))],
        out_specs=tok,
        out_shape=jax.ShapeDtypeStruct((bsz, seq, D_MODEL), F32),
        compiler_params=_params(),
        name="mlp",
    )(x, lw["g_mlp"], g_final, lw["w_up"], lw["w_down"])


def _attention_bias():
    q = jnp.arange(ATTN_BLOCK)[:, None]
    k = jnp.arange(3 * ATTN_BLOCK)[None, :]
    dist = jnp.abs(k - ATTN_BLOCK - q).astype(F32)
    slopes = jnp.exp2(-8.0 * (jnp.arange(ATTN_HEADS, dtype=F32) + 1.0) / ATTN_HEADS)
    bias = jnp.where(dist <= WINDOW, -slopes[:, None, None] * dist, -jnp.inf)
    return bias.reshape(ATTN_KV_HEADS, ATTN_GROUP * ATTN_BLOCK, 3 * ATTN_BLOCK)


def _layer_weights(l, w_in, conv_w, attn_sink, mlstm_gate_b, mlstm_norm_g, w_out_a, w_out_b,
                   w_out_c, w_o, norm_mix_g, norm_mlp_g, w_mlp_up, w_mlp_down):
    n_main = C_MERGE
    w = w_in[l]
    w_proj = jnp.concatenate(
        [w[:, :n_main], w[:, n_main + N_GATES:], w[:, n_main:n_main + N_GATES],
         jnp.zeros((D_MODEL, LANES - N_GATES), F32)], axis=1).astype(BF16)
    gate_b = jnp.concatenate([mlstm_gate_b[l], jnp.zeros((LANES - N_GATES,), F32)])[None, :]
    return {
        "w_proj": w_proj,
        "conv_w": conv_w[l],
        "sink": attn_sink[l],
        "gate_b": gate_b,
        "mnorm_g": mlstm_norm_g[l][None, :],
        "w_a": w_out_a[l].astype(BF16),
        "w_b": w_out_b[l].astype(BF16),
        "w_c": w_out_c[l].astype(BF16),
        "w_o": w_o[l].astype(BF16),
        "g_mix": norm_mix_g[l][None, :],
        "g_mlp": norm_mlp_g[l][None, :],
        "w_up": w_mlp_up[l].astype(BF16),
        "w_down": w_mlp_down[l].astype(BF16),
    }


def kernel(x_prompt, x_sample, w_in, conv_w, attn_sink, mlstm_gate_b, mlstm_norm_g, w_out_a,
           w_out_b, w_out_c, w_o, norm_mix_g, norm_mlp_g, w_mlp_up, w_mlp_down, norm_final_g):
    depth = w_in.shape[0]
    layers = [_layer_weights(l, w_in, conv_w, attn_sink, mlstm_gate_b, mlstm_norm_g, w_out_a,
                             w_out_b, w_out_c, w_o, norm_mix_g, norm_mlp_g, w_mlp_up, w_mlp_down)
              for l in range(depth)]
    attn_bias = _attention_bias()
    g_final = norm_final_g[None, :]

    def trunk(x):
        for l, lw in enumerate(layers):
            x = _mix_out(x, _mix_in(x, lw), lw, attn_bias)
            x = _mlp(x, lw, g_final, final_norm=(l == depth - 1))
        return x

    return (trunk(x_prompt), trunk(x_sample))
```

```python
import functools

import jax
import jax.numpy as jnp
from jax import lax
from jax.experimental import pallas as pl
from jax.experimental.pallas import tpu as pltpu

F32 = jnp.float32
BF16 = jnp.bfloat16

LANES = 128
SUBLANES = 8
VMEM_LIMIT_BYTES = 56 * 1024 * 1024

D_MODEL = 1024
CONV_DIM = 512
ATTN_HEADS = 8
ATTN_KV_HEADS = 2
ATTN_GROUP = ATTN_HEADS // ATTN_KV_HEADS
ATTN_HEAD_DIM = 64
ATTN_DIM = ATTN_HEADS * ATTN_HEAD_DIM
ATTN_KV_DIM = ATTN_KV_HEADS * ATTN_HEAD_DIM
WINDOW = 128
ATTN_BLOCK = 128
MLSTM_HEADS = 4
MLSTM_HEAD_DIM = 128
MLSTM_DIM = MLSTM_HEADS * MLSTM_HEAD_DIM
MLSTM_CHUNK = 128
M_INIT = -1e30
D_FF = 4 * D_MODEL
RMS_EPS = 1e-6
N_GATES = 4 * MLSTM_HEADS

C_CONV = 0
C_AQ = C_CONV + 3 * CONV_DIM
C_AK = C_AQ + ATTN_DIM
C_AV = C_AK + ATTN_KV_DIM
C_ML = C_AV + ATTN_KV_DIM
C_MERGE = C_ML + 4 * MLSTM_DIM
C_GATE = C_MERGE + 3 * D_MODEL
N_PROJ = C_GATE + LANES

TILE = 512
FF_CHUNK = 1024


def _dot(a, b):
    return jnp.dot(a, b, preferred_element_type=F32)


def _dot_nt(a, b):
    return lax.dot_general(a, b, (((1,), (1,)), ((), ())), preferred_element_type=F32)


def _rmsnorm(x, g):
    r = lax.rsqrt(jnp.mean(x * x, axis=-1, keepdims=True) + RMS_EPS)
    return (x * r) * g


def _log_sigmoid(x):
    return jnp.minimum(x, 0.0) - jnp.log(1.0 + jnp.exp(-jnp.abs(x)))


def _mlstm_tile(qkv_ref, gate_rows, st_ref, m_ref, reverse, emit):
    L = MLSTM_CHUNK
    dh = MLSTM_HEAD_DIM
    n_chunks = len(gate_rows)
    order = list(reversed(range(n_chunks))) if reverse else list(range(n_chunks))
    row = lax.broadcasted_iota(jnp.int32, (L, L), 0)
    col = lax.broadcasted_iota(jnp.int32, (L, L), 1)
    keep = (col >= row) if reverse else (col <= row)
    tri = jnp.where((row >= col) if reverse else (row <= col), 1.0, 0.0).astype(F32)
    ones = jnp.ones((L, dh), BF16)

    stats = {}
    m_prev = m_ref[...]
    for c in order:
        gates = gate_rows[c]
        logf = _log_sigmoid(pltpu.roll(gates, MLSTM_HEADS, axis=0))
        b = jnp.dot(logf, tri, precision=lax.Precision.HIGHEST, preferred_element_type=F32)
        r = gates - b
        g = b[:, 0:1] if reverse else b[:, L - 1:L]
        w_state = g + r
        m_loc = jnp.max(w_state, axis=1, keepdims=True)
        m_new = jnp.maximum(g + m_prev, m_loc)
        stats[c] = (logf, r, jnp.exp(w_state - m_loc), m_prev,
                    jnp.exp(g + m_prev - m_new), jnp.exp(m_loc - m_new))
        m_prev = m_new
    m_ref[...] = m_prev

    work = {}
    for c in order:
        logf, r, e_state, m_in, _, _ = stats[c]
        rows = slice(c * L, (c + 1) * L)
        for h in range(MLSTM_HEADS):
            qh = qkv_ref[rows, h * dh:(h + 1) * dh]
            kh = qkv_ref[rows, MLSTM_DIM + h * dh:MLSTM_DIM + (h + 1) * dh]
            vh = qkv_ref[rows, 2 * MLSTM_DIM + h * dh:2 * MLSTM_DIM + (h + 1) * dh]
            rm = jnp.where(keep, r[h:h + 1, :], -jnp.inf)
            b_t = jnp.sum(jnp.where(keep, logf[h:h + 1, :], 0.0), axis=1, keepdims=True)
            mp = m_in[h:h + 1, 0:1]
            c_t = jnp.maximum(jnp.max(rm, axis=1, keepdims=True), mp)
            p = jnp.exp(rm - c_t) * _dot_nt(qh, kh)
            a_t = jnp.exp(mp - c_t)
            lhs = jnp.concatenate([p.astype(BF16), (a_t * qh.astype(F32)).astype(BF16)], axis=1)
            v1 = jnp.concatenate([vh, ones], axis=1)
            ke_t = (jnp.transpose(kh.astype(F32)) * e_state[h:h + 1, :]).astype(BF16)
            work[c, h] = (lhs, v1, jnp.exp(-(b_t + c_t)), _dot(ke_t, v1))

    state = [st_ref[h] for h in range(MLSTM_HEADS)]
    for c in order:
        _, _, _, _, a_vec, c_vec = stats[c]
        for h in range(MLSTM_HEADS):
            lhs, v1, bound, upd = work[c, h]
            tot = _dot(lhs, jnp.concatenate([v1, state[h].astype(BF16)], axis=0))
            emit(c, h, tot[:, :dh] / jnp.maximum(jnp.abs(tot[:, dh:]), bound))
            state[h] = a_vec[h:h + 1, 0:1] * state[h] + c_vec[h:h + 1, 0:1] * upd
    for h in range(MLSTM_HEADS):
        st_ref[h] = state[h]


def _init_state(st_ref, m_ref):
    st_ref[...] = jnp.zeros(st_ref.shape, F32)
    m_ref[...] = jnp.full(m_ref.shape, M_INIT, F32)


def _mix_in_kernel(x_ref, xp_ref, xn_ref, gmix_ref, w_ref, convw_ref, gateb_ref, wa_ref,
                   pa_ref, gbc_ref, aq_ref, kd_ref, vd_ref, mqkv_ref, smo_ref, grow_ref, hf_ref,
                   st_scr, m_scr, *, tile):
    i = pl.program_id(1)
    n_tiles = pl.num_programs(1)

    @pl.when(i == 0)
    def _():
        _init_state(st_scr, m_scr)

    g_mix = gmix_ref[...]
    h = _rmsnorm(x_ref[...], g_mix).astype(BF16)

    ul = _dot(h, w_ref[:, C_ML:C_ML + 4 * MLSTM_DIM])
    mqkv_ref[:, :MLSTM_DIM] = ul[:, :MLSTM_DIM].astype(BF16)
    mqkv_ref[:, MLSTM_DIM:2 * MLSTM_DIM] = (
        ul[:, MLSTM_DIM:2 * MLSTM_DIM] * (MLSTM_HEAD_DIM ** -0.5)).astype(BF16)
    mqkv_ref[:, 2 * MLSTM_DIM:] = ul[:, 2 * MLSTM_DIM:3 * MLSTM_DIM].astype(BF16)
    smo_ref[...] = jax.nn.sigmoid(ul[:, 3 * MLSTM_DIM:]).astype(BF16)
    gcol = _dot(h, w_ref[:, C_GATE:C_GATE + LANES]) + gateb_ref[...]
    gate_rows = []
    for c in range(tile // MLSTM_CHUNK):
        rows = slice(c * MLSTM_CHUNK, (c + 1) * MLSTM_CHUNK)
        g_rows = jnp.transpose(gcol[rows, :])[:2 * SUBLANES, :]
        grow_ref[:, rows] = g_rows
        gate_rows.append(g_rows[:SUBLANES, :])

    def emit_fwd(c, hd, out):
        hf_ref[c * MLSTM_CHUNK:(c + 1) * MLSTM_CHUNK, hd * MLSTM_HEAD_DIM:(hd + 1) * MLSTM_HEAD_DIM] = out

    _mlstm_tile(mqkv_ref, gate_rows, st_scr, m_scr, False, emit_fwd)

    halo = jnp.concatenate([_rmsnorm(xp_ref[...], g_mix), _rmsnorm(xn_ref[...], g_mix)], axis=0)
    uh = _dot(halo.astype(BF16), w_ref[:, C_CONV + CONV_DIM:C_CONV + 3 * CONV_DIM])
    zh = uh[:, :CONV_DIM] * uh[:, CONV_DIM:]
    z_prev = jnp.where(i > 0, zh[SUBLANES - 1:SUBLANES, :], 0.0)
    z_next = jnp.where(i < n_tiles - 1, zh[SUBLANES:SUBLANES + 1, :], 0.0)
    uc = _dot(h, w_ref[:, C_CONV:C_CONV + 3 * CONV_DIM])
    z = uc[:, CONV_DIM:2 * CONV_DIM] * uc[:, 2 * CONV_DIM:]
    trow = lax.broadcasted_iota(jnp.int32, (tile, CONV_DIM), 0)
    z_dn = jnp.where(trow == 0, z_prev, pltpu.roll(z, 1, axis=0))
    z_up = jnp.where(trow == tile - 1, z_next, pltpu.roll(z, tile - 1, axis=0))
    conv = convw_ref[0:1, :] * z_dn + convw_ref[1:2, :] * z + convw_ref[2:3, :] * z_up
    y_a = _dot((uc[:, :CONV_DIM] * conv).astype(BF16), wa_ref[...])

    um = _dot(h, w_ref[:, C_MERGE:C_MERGE + 3 * D_MODEL])
    pa_ref[...] = (jax.nn.sigmoid(um[:, :D_MODEL]) * y_a).astype(BF16)
    gbc_ref[...] = jax.nn.sigmoid(um[:, D_MODEL:]).astype(BF16)

    ua = _dot(h, w_ref[:, C_AQ:C_AV + ATTN_KV_DIM])
    aq_ref[...] = (ua[:, :ATTN_DIM] * (ATTN_HEAD_DIM ** -0.5)).astype(BF16)
    lane = lax.broadcasted_iota(jnp.int32, (tile, LANES), 1)
    low = lane < ATTN_HEAD_DIM
    for src, dst in ((ua[:, ATTN_DIM:ATTN_DIM + ATTN_KV_DIM], kd_ref),
                     (ua[:, ATTN_DIM + ATTN_KV_DIM:], vd_ref)):
        swapped = pltpu.roll(src, ATTN_HEAD_DIM, axis=1)
        dst[:, :LANES] = jnp.where(low, src, swapped).astype(BF16)
        dst[:, LANES:] = jnp.where(low, swapped, src).astype(BF16)


def _mix_out_kernel(sink_ref, x_ref, pa_ref, gbc_ref, aq_ref, kd_ref, kdp_ref, kdn_ref,
                    vd_ref, vdp_ref, vdn_ref, mqkv_ref, smo_ref, grow_ref, hf_ref, bias_ref,
                    mnorm_ref, wb_ref, wc_ref, wo_ref,
                    xo_ref,
                    kwin, vwin, att_scr, yc_scr, st_scr, m_scr, *, tile):
    i = pl.program_id(1)
    n_tiles = pl.num_programs(1)
    ti = n_tiles - 1 - i
    n_qblk = tile // ATTN_BLOCK

    @pl.when(i == 0)
    def _():
        _init_state(st_scr, m_scr)

    kwin[0:ATTN_BLOCK, :] = kdp_ref[...]
    kwin[ATTN_BLOCK:ATTN_BLOCK + tile, :] = kd_ref[...]
    kwin[ATTN_BLOCK + tile:, :] = kdn_ref[...]
    vwin[0:ATTN_BLOCK, :] = vdp_ref[...]
    vwin[ATTN_BLOCK:ATTN_BLOCK + tile, :] = vd_ref[...]
    vwin[ATTN_BLOCK + tile:, :] = vdn_ref[...]

    lane = lax.broadcasted_iota(jnp.int32, (ATTN_BLOCK, LANES), 1)
    low = lane < ATTN_HEAD_DIM
    kcol = lax.broadcasted_iota(jnp.int32, (1, 3 * ATTN_BLOCK), 1)

    for j in range(n_qblk):
        rows = slice(j * ATTN_BLOCK, (j + 1) * ATTN_BLOCK)
        wrows = slice(j * ATTN_BLOCK, (j + 3) * ATTN_BLOCK)
        outside = None
        if j == 0:
            outside = jnp.logical_and(kcol < ATTN_BLOCK, ti == 0)
        if j == n_qblk - 1:
            after = jnp.logical_and(kcol >= 2 * ATTN_BLOCK, ti == n_tiles - 1)
            outside = after if outside is None else jnp.logical_or(outside, after)
        for g in range(ATTN_KV_HEADS):
            kw = kwin[wrows, g * LANES:(g + 1) * LANES]
            vw = vwin[wrows, g * LANES:(g + 1) * LANES]
            qs = []
            for pr in range(2):
                pair = g * 2 + pr
                qp = aq_ref[rows, pair * LANES:(pair + 1) * LANES]
                qs.append(jnp.where(low, qp, jnp.zeros_like(qp)))
                qs.append(jnp.where(low, jnp.zeros_like(qp), qp))
            bias = bias_ref[g]
            if outside is not None:
                bias = bias + jnp.where(outside, -jnp.inf, 0.0)
            s_all = _dot_nt(jnp.concatenate(qs, axis=0), kw) + bias
            outs = []
            for idx in range(ATTN_GROUP):
                sk = sink_ref[g * ATTN_GROUP + idx]
                s = s_all[idx * ATTN_BLOCK:(idx + 1) * ATTN_BLOCK, :]
                m = jnp.maximum(jnp.max(s, axis=1, keepdims=True), sk)
                p = jnp.exp(s - m)
                denom = jnp.sum(p, axis=1, keepdims=True) + jnp.exp(sk - m)
                outs.append(_dot(p.astype(BF16), vw) / denom)
            for pr in range(2):
                pair = g * 2 + pr
                att_scr[rows, pair * LANES:(pair + 1) * LANES] = jnp.where(
                    low, outs[2 * pr], outs[2 * pr + 1]).astype(BF16)

    def emit_bwd(c, hd, out):
        rows = slice(c * MLSTM_CHUNK, (c + 1) * MLSTM_CHUNK)
        sl = slice(hd * MLSTM_HEAD_DIM, (hd + 1) * MLSTM_HEAD_DIM)
        hm = hf_ref[rows, sl] + out
        hm = hm * lax.rsqrt(jnp.mean(hm * hm, axis=-1, keepdims=True) + RMS_EPS)
        yc_scr[rows, sl] = (smo_ref[rows, sl].astype(F32) * (hm * mnorm_ref[:, sl])).astype(BF16)

    gate_rows = [grow_ref[SUBLANES:, c * MLSTM_CHUNK:(c + 1) * MLSTM_CHUNK]
                 for c in range(tile // MLSTM_CHUNK)]
    _mlstm_tile(mqkv_ref, gate_rows, st_scr, m_scr, True, emit_bwd)

    y_b = _dot(att_scr[...], wb_ref[...])
    y_c = _dot(yc_scr[...], wc_ref[...])
    merged = (pa_ref[...].astype(F32) + gbc_ref[:, :D_MODEL].astype(F32) * y_b
              + gbc_ref[:, D_MODEL:].astype(F32) * y_c)
    xo_ref[...] = x_ref[...] + _dot(merged.astype(BF16), wo_ref[...])


def _mlp_kernel(x_ref, g_ref, gf_ref, wup_ref, wdn_ref, o_ref, *, final_norm):
    x = x_ref[...]
    h = _rmsnorm(x, g_ref[...]).astype(BF16)
    acc = x
    for c in range(D_FF // FF_CHUNK):
        cols = slice(c * FF_CHUNK, (c + 1) * FF_CHUNK)
        up = jnp.maximum(_dot(h, wup_ref[:, cols]), 0.0)
        acc = acc + _dot((up * up).astype(BF16), wdn_ref[cols, :])
    if final_norm:
        acc = _rmsnorm(acc, gf_ref[...])
    o_ref[...] = acc


def _resident(shape):
    nd = len(shape)
    return pl.BlockSpec(shape, lambda b, i: (0,) * nd, pipeline_mode=pl.Buffered(1))


def _params():
    return pltpu.CompilerParams(dimension_semantics=("arbitrary", "arbitrary"),
                                vmem_limit_bytes=VMEM_LIMIT_BYTES)


def _mlstm_state_scratch():
    return [pltpu.VMEM((MLSTM_HEADS, MLSTM_HEAD_DIM, 2 * MLSTM_HEAD_DIM), F32),
            pltpu.VMEM((SUBLANES, LANES), F32)]


def _mix_in(x, lw):
    bsz, seq, _ = x.shape
    tile = TILE
    n_tiles = seq // tile
    halo_per_tile = tile // SUBLANES
    n_halo = seq // SUBLANES

    def tok(width):
        return pl.BlockSpec((None, tile, width), lambda b, i: (b, i, 0))

    in_specs = [
        tok(D_MODEL),
        pl.BlockSpec((None, SUBLANES, D_MODEL),
                     lambda b, i: (b, jnp.maximum(i * halo_per_tile - 1, 0), 0)),
        pl.BlockSpec((None, SUBLANES, D_MODEL),
                     lambda b, i: (b, jnp.minimum((i + 1) * halo_per_tile, n_halo - 1), 0)),
        _resident((1, D_MODEL)),
        _resident((D_MODEL, N_PROJ)),
        _resident((3, CONV_DIM)),
        _resident((1, LANES)),
        _resident((CONV_DIM, D_MODEL)),
    ]
    widths = (D_MODEL, 2 * D_MODEL, ATTN_DIM, 2 * LANES, 2 * LANES, 3 * MLSTM_DIM, MLSTM_DIM)
    out_shape = [jax.ShapeDtypeStruct((bsz, seq, w), BF16) for w in widths]
    out_shape += [jax.ShapeDtypeStruct((bsz, 2 * SUBLANES, seq), F32),
                  jax.ShapeDtypeStruct((bsz, seq, MLSTM_DIM), F32)]
    out_specs = [tok(w) for w in widths]
    out_specs += [pl.BlockSpec((None, 2 * SUBLANES, tile), lambda b, i: (b, 0, i)), tok(MLSTM_DIM)]
    return pl.pallas_call(
        functools.partial(_mix_in_kernel, tile=tile),
        grid=(bsz, n_tiles),
        in_specs=in_specs,
        out_specs=out_specs,
        out_shape=out_shape,
        scratch_shapes=_mlstm_state_scratch(),
        compiler_params=_params(),
        name="mix_in",
    )(x, x, x, lw["g_mix"], lw["w_proj"], lw["conv_w"], lw["gate_b"], lw["w_a"])


def _mix_out(x, mid, lw, attn_bias):
    pa, gbc, aq, kd, vd, mqkv, smo, grow, hf = mid
    bsz, seq, _ = x.shape
    tile = TILE
    n_tiles = seq // tile
    blk_per_tile = tile // ATTN_BLOCK
    n_blk = seq // ATTN_BLOCK

    def tok(width):
        return pl.BlockSpec((None, tile, width), lambda b, i: (b, n_tiles - 1 - i, 0))

    def blk_prev(width):
        return pl.BlockSpec(
            (None, ATTN_BLOCK, width),
            lambda b, i: (b, jnp.maximum((n_tiles - 1 - i) * blk_per_tile - 1, 0), 0))

    def blk_next(width):
        return pl.BlockSpec(
            (None, ATTN_BLOCK, width),
            lambda b, i: (b, jnp.minimum((n_tiles - i) * blk_per_tile, n_blk - 1), 0))

    in_specs = [
        pl.BlockSpec(memory_space=pltpu.SMEM),
        tok(D_MODEL), tok(D_MODEL), tok(2 * D_MODEL), tok(ATTN_DIM),
        tok(2 * LANES), blk_prev(2 * LANES), blk_next(2 * LANES),
        tok(2 * LANES), blk_prev(2 * LANES), blk_next(2 * LANES),
        tok(3 * MLSTM_DIM), tok(MLSTM_DIM),
        pl.BlockSpec((None, 2 * SUBLANES, tile), lambda b, i: (b, 0, n_tiles - 1 - i)),
        tok(MLSTM_DIM),
        _resident((ATTN_KV_HEADS, ATTN_GROUP * ATTN_BLOCK, 3 * ATTN_BLOCK)),
        _resident((1, MLSTM_DIM)),
        _resident((ATTN_DIM, D_MODEL)),
        _resident((MLSTM_DIM, D_MODEL)),
        _resident((D_MODEL, D_MODEL)),
    ]
    scratch = [pltpu.VMEM((tile + 2 * ATTN_BLOCK, 2 * LANES), BF16),
               pltpu.VMEM((tile + 2 * ATTN_BLOCK, 2 * LANES), BF16),
               pltpu.VMEM((tile, ATTN_DIM), BF16),
               pltpu.VMEM((tile, MLSTM_DIM), BF16)]
    return pl.pallas_call(
        functools.partial(_mix_out_kernel, tile=tile),
        grid=(bsz, n_tiles),
        in_specs=in_specs,
        out_specs=tok(D_MODEL),
        out_shape=jax.ShapeDtypeStruct((bsz, seq, D_MODEL), F32),
        scratch_shapes=scratch + _mlstm_state_scratch(),
        compiler_params=_params(),
        name="mix_out",
    )(lw["sink"], x, pa, gbc, aq, kd, kd, kd, vd, vd, vd, mqkv, smo, grow, hf, attn_bias,
      lw["mnorm_g"], lw["w_b"], lw["w_c"], lw["w_o"])


def _mlp(x, lw, g_final, final_norm):
    bsz, seq, _ = x.shape
    tile = TILE
    tok = pl.BlockSpec((None, tile, D_MODEL), lambda b, i: (b, i, 0))
    return pl.pallas_call(
        functools.partial(_mlp_kernel, final_norm=final_norm),
        grid=(bsz, seq // tile),
        in_specs=[tok, _resident((1, D_MODEL)), _resident((1, D_MODEL)),
                  _resident((D_MODEL, D_FF)), _resident((D_FF, D_MODEL))],
        out_specs=tok,
        out_shape=jax.ShapeDtypeStruct((bsz, seq, D_MODEL), F32),
        compiler_params=_params(),
        name="mlp",
    )(x, lw["g_mlp"], g_final, lw["w_up"], lw["w_down"])


def _attention_bias():
    q = jnp.arange(ATTN_BLOCK)[:, None]
    k = jnp.arange(3 * ATTN_BLOCK)[None, :]
    dist = jnp.abs(k - ATTN_BLOCK - q).astype(F32)
    slopes = jnp.exp2(-8.0 * (jnp.arange(ATTN_HEADS, dtype=F32) + 1.0) / ATTN_HEADS)
    bias = jnp.where(dist <= WINDOW, -slopes[:, None, None] * dist, -jnp.inf)
    return bias.reshape(ATTN_KV_HEADS, ATTN_GROUP * ATTN_BLOCK, 3 * ATTN_BLOCK)


def _layer_weights(l, w_in, conv_w, attn_sink, mlstm_gate_b, mlstm_norm_g, w_out_a, w_out_b,
                   w_out_c, w_o, norm_mix_g, norm_mlp_g, w_mlp_up, w_mlp_down):
    n_main = C_MERGE
    w = w_in[l]
    w_proj = jnp.concatenate(
        [w[:, :n_main], w[:, n_main + N_GATES:], w[:, n_main:n_main + N_GATES],
         jnp.zeros((D_MODEL, LANES - N_GATES), F32)], axis=1).astype(BF16)
    gate_b = jnp.concatenate([mlstm_gate_b[l], jnp.zeros((LANES - N_GATES,), F32)])[None, :]
    return {
        "w_proj": w_proj,
        "conv_w": conv_w[l],
        "sink": attn_sink[l],
        "gate_b": gate_b,
        "mnorm_g": mlstm_norm_g[l][None, :],
        "w_a": w_out_a[l].astype(BF16),
        "w_b": w_out_b[l].astype(BF16),
        "w_c": w_out_c[l].astype(BF16),
        "w_o": w_o[l].astype(BF16),
        "g_mix": norm_mix_g[l][None, :],
        "g_mlp": norm_mlp_g[l][None, :],
        "w_up": w_mlp_up[l].astype(BF16),
        "w_down": w_mlp_down[l].astype(BF16),
    }


def kernel(x_prompt, x_sample, w_in, conv_w, attn_sink, mlstm_gate_b, mlstm_norm_g, w_out_a,
           w_out_b, w_out_c, w_o, norm_mix_g, norm_mlp_g, w_mlp_up, w_mlp_down, norm_final_g):
    depth = w_in.shape[0]
    layers = [_layer_weights(l, w_in, conv_w, attn_sink, mlstm_gate_b, mlstm_norm_g, w_out_a,
                             w_out_b, w_out_c, w_o, norm_mix_g, norm_mlp_g, w_mlp_up, w_mlp_down)
              for l in range(depth)]
    attn_bias = _attention_bias()
    g_final = norm_final_g[None, :]

    def trunk(x):
        for l, lw in enumerate(layers):
            x = _mix_out(x, _mix_in(x, lw), lw, attn_bias)
            x = _mlp(x, lw, g_final, final_norm=(l == depth - 1))
        return x

    return (trunk(x_prompt), trunk(x_sample))
```

```python
import functools

import jax
import jax.numpy as jnp
from jax import lax
from jax.experimental import pallas as pl
from jax.experimental.pallas import tpu as pltpu

F32 = jnp.float32
BF16 = jnp.bfloat16

LANES = 128
SUBLANES = 8
VMEM_LIMIT_BYTES = 56 * 1024 * 1024

D_MODEL = 1024
CONV_DIM = 512
ATTN_HEADS = 8
ATTN_KV_HEADS = 2
ATTN_GROUP = ATTN_HEADS // ATTN_KV_HEADS
ATTN_HEAD_DIM = 64
ATTN_DIM = ATTN_HEADS * ATTN_HEAD_DIM
ATTN_KV_DIM = ATTN_KV_HEADS * ATTN_HEAD_DIM
WINDOW = 128
ATTN_BLOCK = 128
MLSTM_HEADS = 4
MLSTM_HEAD_DIM = 128
MLSTM_DIM = MLSTM_HEADS * MLSTM_HEAD_DIM
MLSTM_CHUNK = 128
M_INIT = -1e30
D_FF = 4 * D_MODEL
RMS_EPS = 1e-6
N_GATES = 4 * MLSTM_HEADS

C_CONV = 0
C_AQ = C_CONV + 3 * CONV_DIM
C_AK = C_AQ + ATTN_DIM
C_AV = C_AK + ATTN_KV_DIM
C_ML = C_AV + ATTN_KV_DIM
C_MERGE = C_ML + 4 * MLSTM_DIM
C_GATE = C_MERGE + 3 * D_MODEL
N_PROJ = C_GATE + LANES

TILE = 512
MLP_TILE = 1024
FF_CHUNK = 1024
MERGE_BLOCK = 512
MLSTM_PIECES_PER_PROJ = 5


def _dot(a, b):
    return jnp.dot(a, b, preferred_element_type=F32)


def _dot_nt(a, b):
    return lax.dot_general(a, b, (((1,), (1,)), ((), ())), preferred_element_type=F32)


def _rmsnorm(x, g):
    r = lax.rsqrt(jnp.mean(x * x, axis=-1, keepdims=True) + RMS_EPS)
    return (x * r) * g


def _log_sigmoid(x):
    return jnp.minimum(x, 0.0) - jnp.log(1.0 + jnp.exp(-jnp.abs(x)))


def _mlstm_tile(qkv_ref, gate_rows, st_ref, m_ref, reverse, emit):
    L = MLSTM_CHUNK
    dh = MLSTM_HEAD_DIM
    n_chunks = len(gate_rows)
    order = list(reversed(range(n_chunks))) if reverse else list(range(n_chunks))
    row = lax.broadcasted_iota(jnp.int32, (L, L), 0)
    col = lax.broadcasted_iota(jnp.int32, (L, L), 1)
    keep = (col >= row) if reverse else (col <= row)
    tri = jnp.where((row >= col) if reverse else (row <= col), 1.0, 0.0).astype(F32)
    ones = jnp.ones((L, dh), BF16)

    stats = {}
    m_prev = m_ref[...]
    for c in order:
        gates = gate_rows[c]
        logf = _log_sigmoid(pltpu.roll(gates, MLSTM_HEADS, axis=0))
        b = jnp.dot(logf, tri, precision=lax.Precision.HIGHEST, preferred_element_type=F32)
        r = gates - b
        g = b[:, 0:1] if reverse else b[:, L - 1:L]
        w_state = g + r
        m_loc = jnp.max(w_state, axis=1, keepdims=True)
        m_new = jnp.maximum(g + m_prev, m_loc)
        stats[c] = (logf, r, jnp.exp(w_state - m_loc), m_prev,
                    jnp.exp(g + m_prev - m_new), jnp.exp(m_loc - m_new))
        m_prev = m_new
    m_ref[...] = m_prev
    yield

    work = {}
    for c in order:
        logf, r, e_state, m_in, _, _ = stats[c]
        rows = slice(c * L, (c + 1) * L)
        for h in range(MLSTM_HEADS):
            qh = qkv_ref[rows, h * dh:(h + 1) * dh]
            kh = qkv_ref[rows, MLSTM_DIM + h * dh:MLSTM_DIM + (h + 1) * dh]
            vh = qkv_ref[rows, 2 * MLSTM_DIM + h * dh:2 * MLSTM_DIM + (h + 1) * dh]
            rm = jnp.where(keep, r[h:h + 1, :], -jnp.inf)
            b_t = jnp.sum(jnp.where(keep, logf[h:h + 1, :], 0.0), axis=1, keepdims=True)
            mp = m_in[h:h + 1, 0:1]
            c_t = jnp.maximum(jnp.max(rm, axis=1, keepdims=True), mp)
            p = jnp.exp(rm - c_t) * _dot_nt(qh, kh)
            a_t = jnp.exp(mp - c_t)
            lhs = jnp.concatenate([p.astype(BF16), (a_t * qh.astype(F32)).astype(BF16)], axis=1)
            v1 = jnp.concatenate([vh, ones], axis=1)
            ke_t = (jnp.transpose(kh.astype(F32)) * e_state[h:h + 1, :]).astype(BF16)
            work[c, h] = (lhs, v1, jnp.exp(-(b_t + c_t)), _dot(ke_t, v1))
            yield

    state = [st_ref[h] for h in range(MLSTM_HEADS)]
    for c in order:
        _, _, _, _, a_vec, c_vec = stats[c]
        for h in range(MLSTM_HEADS):
            lhs, v1, bound, upd = work[c, h]
            tot = _dot(lhs, jnp.concatenate([v1, state[h].astype(BF16)], axis=0))
            emit(c, h, tot[:, :dh] / jnp.maximum(jnp.abs(tot[:, dh:]), bound))
            state[h] = a_vec[h:h + 1, 0:1] * state[h] + c_vec[h:h + 1, 0:1] * upd
            yield
    for h in range(MLSTM_HEADS):
        st_ref[h] = state[h]


def _init_state(st_ref, m_ref):
    st_ref[...] = jnp.zeros(st_ref.shape, F32)
    m_ref[...] = jnp.full(m_ref.shape, M_INIT, F32)


def _mix_in_kernel(x_ref, xp_ref, xn_ref, gmix_ref, w_ref, convw_ref, gateb_ref, wa_ref,
                   pa_ref, gbc_ref, aq_ref, kd_ref, vd_ref, mqkv_ref, smo_ref, grow_ref, hf_ref,
                   st_scr, m_scr, *, tile):
    i = pl.program_id(1)
    n_tiles = pl.num_programs(1)

    @pl.when(i == 0)
    def _():
        _init_state(st_scr, m_scr)

    g_mix = gmix_ref[...]
    h = _rmsnorm(x_ref[...], g_mix).astype(BF16)

    ul = _dot(h, w_ref[:, C_ML:C_ML + 4 * MLSTM_DIM])
    mqkv_ref[:, :MLSTM_DIM] = ul[:, :MLSTM_DIM].astype(BF16)
    mqkv_ref[:, MLSTM_DIM:2 * MLSTM_DIM] = (
        ul[:, MLSTM_DIM:2 * MLSTM_DIM] * (MLSTM_HEAD_DIM ** -0.5)).astype(BF16)
    mqkv_ref[:, 2 * MLSTM_DIM:] = ul[:, 2 * MLSTM_DIM:3 * MLSTM_DIM].astype(BF16)
    smo_ref[...] = jax.nn.sigmoid(ul[:, 3 * MLSTM_DIM:]).astype(BF16)
    gcol = _dot(h, w_ref[:, C_GATE:C_GATE + LANES]) + gateb_ref[...]
    gate_rows = []
    for c in range(tile // MLSTM_CHUNK):
        rows = slice(c * MLSTM_CHUNK, (c + 1) * MLSTM_CHUNK)
        g_rows = jnp.transpose(gcol[rows, :])[:2 * SUBLANES, :]
        grow_ref[:, rows] = g_rows
        gate_rows.append(g_rows[:SUBLANES, :])

    def emit_fwd(c, hd, out):
        hf_ref[c * MLSTM_CHUNK:(c + 1) * MLSTM_CHUNK, hd * MLSTM_HEAD_DIM:(hd + 1) * MLSTM_HEAD_DIM] = out

    mlstm = _mlstm_tile(mqkv_ref, gate_rows, st_scr, m_scr, False, emit_fwd)

    def conv_proj():
        halo = jnp.concatenate([_rmsnorm(xp_ref[...], g_mix), _rmsnorm(xn_ref[...], g_mix)], axis=0)
        uh = _dot(halo.astype(BF16), w_ref[:, C_CONV + CONV_DIM:C_CONV + 3 * CONV_DIM])
        zh = uh[:, :CONV_DIM] * uh[:, CONV_DIM:]
        z_prev = jnp.where(i > 0, zh[SUBLANES - 1:SUBLANES, :], 0.0)
        z_next = jnp.where(i < n_tiles - 1, zh[SUBLANES:SUBLANES + 1, :], 0.0)
        return _dot(h, w_ref[:, C_CONV:C_CONV + 3 * CONV_DIM]), z_prev, z_next

    def conv_out(uc, z_prev, z_next):
        z = uc[:, CONV_DIM:2 * CONV_DIM] * uc[:, 2 * CONV_DIM:]
        trow = lax.broadcasted_iota(jnp.int32, (tile, CONV_DIM), 0)
        z_dn = jnp.where(trow == 0, z_prev, pltpu.roll(z, 1, axis=0))
        z_up = jnp.where(trow == tile - 1, z_next, pltpu.roll(z, tile - 1, axis=0))
        conv = convw_ref[0:1, :] * z_dn + convw_ref[1:2, :] * z + convw_ref[2:3, :] * z_up
        return _dot((uc[:, :CONV_DIM] * conv).astype(BF16), wa_ref[...])

    def merge_gate(y_a, blk):
        cols = slice(blk * MERGE_BLOCK, (blk + 1) * MERGE_BLOCK)
        sg = jax.nn.sigmoid(_dot(h, w_ref[:, C_MERGE + blk * MERGE_BLOCK:C_MERGE + (blk + 1) * MERGE_BLOCK]))
        if blk < D_MODEL // MERGE_BLOCK:
            pa_ref[:, cols] = (sg * y_a[:, cols]).astype(BF16)
        else:
            gbc_ref[:, blk * MERGE_BLOCK - D_MODEL:(blk + 1) * MERGE_BLOCK - D_MODEL] = sg.astype(BF16)

    def attn_in():
        ua = _dot(h, w_ref[:, C_AQ:C_AV + ATTN_KV_DIM])
        aq_ref[...] = (ua[:, :ATTN_DIM] * (ATTN_HEAD_DIM ** -0.5)).astype(BF16)
        lane = lax.broadcasted_iota(jnp.int32, (tile, LANES), 1)
        low = lane < ATTN_HEAD_DIM
        for src, dst in ((ua[:, ATTN_DIM:ATTN_DIM + ATTN_KV_DIM], kd_ref),
                         (ua[:, ATTN_DIM + ATTN_KV_DIM:], vd_ref)):
            swapped = pltpu.roll(src, ATTN_HEAD_DIM, axis=1)
            dst[:, :LANES] = jnp.where(low, src, swapped).astype(BF16)
            dst[:, LANES:] = jnp.where(low, swapped, src).astype(BF16)

    def advance():
        for _ in range(MLSTM_PIECES_PER_PROJ):
            next(mlstm, None)

    n_a_blocks = D_MODEL // MERGE_BLOCK
    advance()
    conv_parts = conv_proj()
    advance()
    merge_gate(None, n_a_blocks)
    advance()
    y_a = conv_out(*conv_parts)
    merge_gate(None, n_a_blocks + 1)
    advance()
    attn_in()
    for blk in range(n_a_blocks + 2, 3 * n_a_blocks):
        merge_gate(None, blk)
        advance()
    for _ in mlstm:
        pass
    for blk in range(n_a_blocks):
        merge_gate(y_a, blk)


def _mix_out_kernel(sink_ref, x_ref, pa_ref, gbc_ref, aq_ref, kd_ref, kdp_ref, kdn_ref,
                    vd_ref, vdp_ref, vdn_ref, mqkv_ref, smo_ref, grow_ref, hf_ref, bias_ref,
                    mnorm_ref, wb_ref, wc_ref, wo_ref,
                    xo_ref,
                    kwin, vwin, att_scr, yc_scr, st_scr, m_scr, *, tile):
    i = pl.program_id(1)
    n_tiles = pl.num_programs(1)
    ti = n_tiles - 1 - i
    n_qblk = tile // ATTN_BLOCK

    @pl.when(i == 0)
    def _():
        _init_state(st_scr, m_scr)

    kwin[0:ATTN_BLOCK, :] = kdp_ref[...]
    kwin[ATTN_BLOCK:ATTN_BLOCK + tile, :] = kd_ref[...]
    kwin[ATTN_BLOCK + tile:, :] = kdn_ref[...]
    vwin[0:ATTN_BLOCK, :] = vdp_ref[...]
    vwin[ATTN_BLOCK:ATTN_BLOCK + tile, :] = vd_ref[...]
    vwin[ATTN_BLOCK + tile:, :] = vdn_ref[...]

    lane = lax.broadcasted_iota(jnp.int32, (ATTN_BLOCK, LANES), 1)
    low = lane < ATTN_HEAD_DIM
    kcol = lax.broadcasted_iota(jnp.int32, (1, 3 * ATTN_BLOCK), 1)

    for j in range(n_qblk):
        rows = slice(j * ATTN_BLOCK, (j + 1) * ATTN_BLOCK)
        wrows = slice(j * ATTN_BLOCK, (j + 3) * ATTN_BLOCK)
        outside = None
        if j == 0:
            outside = jnp.logical_and(kcol < ATTN_BLOCK, ti == 0)
        if j == n_qblk - 1:
            after = jnp.logical_and(kcol >= 2 * ATTN_BLOCK, ti == n_tiles - 1)
            outside = after if outside is None else jnp.logical_or(outside, after)
        for g in range(ATTN_KV_HEADS):
            kw = kwin[wrows, g * LANES:(g + 1) * LANES]
            vw = vwin[wrows, g * LANES:(g + 1) * LANES]
            qs = []
            for pr in range(2):
                pair = g * 2 + pr
                qp = aq_ref[rows, pair * LANES:(pair + 1) * LANES]
                qs.append(jnp.where(low, qp, jnp.zeros_like(qp)))
                qs.append(jnp.where(low, jnp.zeros_like(qp), qp))
            bias = bias_ref[g]
            if outside is not None:
                bias = bias + jnp.where(outside, -jnp.inf, 0.0)
            s_all = _dot_nt(jnp.concatenate(qs, axis=0), kw) + bias
            outs = []
            for idx in range(ATTN_GROUP):
                sk = sink_ref[g * ATTN_GROUP + idx]
                s = s_all[idx * ATTN_BLOCK:(idx + 1) * ATTN_BLOCK, :]
                m = jnp.maximum(jnp.max(s, axis=1, keepdims=True), sk)
                p = jnp.exp(s - m)
                denom = jnp.sum(p, axis=1, keepdims=True) + jnp.exp(sk - m)
                outs.append(_dot(p.astype(BF16), vw) / denom)
            for pr in range(2):
                pair = g * 2 + pr
                att_scr[rows, pair * LANES:(pair + 1) * LANES] = jnp.where(
                    low, outs[2 * pr], outs[2 * pr + 1]).astype(BF16)

    def emit_bwd(c, hd, out):
        rows = slice(c * MLSTM_CHUNK, (c + 1) * MLSTM_CHUNK)
        sl = slice(hd * MLSTM_HEAD_DIM, (hd + 1) * MLSTM_HEAD_DIM)
        hm = hf_ref[rows, sl] + out
        hm = hm * lax.rsqrt(jnp.mean(hm * hm, axis=-1, keepdims=True) + RMS_EPS)
        yc_scr[rows, sl] = (smo_ref[rows, sl].astype(F32) * (hm * mnorm_ref[:, sl])).astype(BF16)

    gate_rows = [grow_ref[SUBLANES:, c * MLSTM_CHUNK:(c + 1) * MLSTM_CHUNK]
                 for c in range(tile // MLSTM_CHUNK)]
    for _ in _mlstm_tile(mqkv_ref, gate_rows, st_scr, m_scr, True, emit_bwd):
        pass

    y_b = _dot(att_scr[...], wb_ref[...])
    y_c = _dot(yc_scr[...], wc_ref[...])
    merged = (pa_ref[...].astype(F32) + gbc_ref[:, :D_MODEL].astype(F32) * y_b
              + gbc_ref[:, D_MODEL:].astype(F32) * y_c)
    xo_ref[...] = x_ref[...] + _dot(merged.astype(BF16), wo_ref[...])


def _mlp_kernel(x_ref, g_ref, gf_ref, wup_ref, wdn_ref, o_ref, *, final_norm):
    x = x_ref[...]
    h = _rmsnorm(x, g_ref[...]).astype(BF16)
    acc = x
    for c in range(D_FF // FF_CHUNK):
        cols = slice(c * FF_CHUNK, (c + 1) * FF_CHUNK)
        up = jnp.maximum(_dot(h, wup_ref[:, cols]), 0.0)
        acc = acc + _dot((up * up).astype(BF16), wdn_ref[cols, :])
    if final_norm:
        acc = _rmsnorm(acc, gf_ref[...])
    o_ref[...] = acc


def _resident(shape):
    nd = len(shape)
    return pl.BlockSpec(shape, lambda b, i: (0,) * nd, pipeline_mode=pl.Buffered(1))


def _params():
    return pltpu.CompilerParams(dimension_semantics=("arbitrary", "arbitrary"),
                                vmem_limit_bytes=VMEM_LIMIT_BYTES)


def _mlstm_state_scratch():
    return [pltpu.VMEM((MLSTM_HEADS, MLSTM_HEAD_DIM, 2 * MLSTM_HEAD_DIM), F32),
            pltpu.VMEM((SUBLANES, LANES), F32)]


def _mix_in(x, lw):
    bsz, seq, _ = x.shape
    tile = TILE
    n_tiles = seq // tile
    halo_per_tile = tile // SUBLANES
    n_halo = seq // SUBLANES

    def tok(width):
        return pl.BlockSpec((None, tile, width), lambda b, i: (b, i, 0))

    in_specs = [
        tok(D_MODEL),
        pl.BlockSpec((None, SUBLANES, D_MODEL),
                     lambda b, i: (b, jnp.maximum(i * halo_per_tile - 1, 0), 0)),
        pl.BlockSpec((None, SUBLANES, D_MODEL),
                     lambda b, i: (b, jnp.minimum((i + 1) * halo_per_tile, n_halo - 1), 0)),
        _resident((1, D_MODEL)),
        _resident((D_MODEL, N_PROJ)),
        _resident((3, CONV_DIM)),
        _resident((1, LANES)),
        _resident((CONV_DIM, D_MODEL)),
    ]
    widths = (D_MODEL, 2 * D_MODEL, ATTN_DIM, 2 * LANES, 2 * LANES, 3 * MLSTM_DIM, MLSTM_DIM)
    out_shape = [jax.ShapeDtypeStruct((bsz, seq, w), BF16) for w in widths]
    out_shape += [jax.ShapeDtypeStruct((bsz, 2 * SUBLANES, seq), F32),
                  jax.ShapeDtypeStruct((bsz, seq, MLSTM_DIM), F32)]
    out_specs = [tok(w) for w in widths]
    out_specs += [pl.BlockSpec((None, 2 * SUBLANES, tile), lambda b, i: (b, 0, i)), tok(MLSTM_DIM)]
    return pl.pallas_call(
        functools.partial(_mix_in_kernel, tile=tile),
        grid=(bsz, n_tiles),
        in_specs=in_specs,
        out_specs=out_specs,
        out_shape=out_shape,
        scratch_shapes=_mlstm_state_scratch(),
        compiler_params=_params(),
        name="mix_in",
    )(x, x, x, lw["g_mix"], lw["w_proj"], lw["conv_w"], lw["gate_b"], lw["w_a"])


def _mix_out(x, mid, lw, attn_bias):
    pa, gbc, aq, kd, vd, mqkv, smo, grow, hf = mid
    bsz, seq, _ = x.shape
    tile = TILE
    n_tiles = seq // tile
    blk_per_tile = tile // ATTN_BLOCK
    n_blk = seq // ATTN_BLOCK

    def tok(width):
        return pl.BlockSpec((None, tile, width), lambda b, i: (b, n_tiles - 1 - i, 0))

    def blk_prev(width):
        return pl.BlockSpec(
            (None, ATTN_BLOCK, width),
            lambda b, i: (b, jnp.maximum((n_tiles - 1 - i) * blk_per_tile - 1, 0), 0))

    def blk_next(width):
        return pl.BlockSpec(
            (None, ATTN_BLOCK, width),
            lambda b, i: (b, jnp.minimum((n_tiles - i) * blk_per_tile, n_blk - 1), 0))

    in_specs = [
        pl.BlockSpec(memory_space=pltpu.SMEM),
        tok(D_MODEL), tok(D_MODEL), tok(2 * D_MODEL), tok(ATTN_DIM),
        tok(2 * LANES), blk_prev(2 * LANES), blk_next(2 * LANES),
        tok(2 * LANES), blk_prev(2 * LANES), blk_next(2 * LANES),
        tok(3 * MLSTM_DIM), tok(MLSTM_DIM),
        pl.BlockSpec((None, 2 * SUBLANES, tile), lambda b, i: (b, 0, n_tiles - 1 - i)),
        tok(MLSTM_DIM),
        _resident((ATTN_KV_HEADS, ATTN_GROUP * ATTN_BLOCK, 3 * ATTN_BLOCK)),
        _resident((1, MLSTM_DIM)),
        _resident((ATTN_DIM, D_MODEL)),
        _resident((MLSTM_DIM, D_MODEL)),
        _resident((D_MODEL, D_MODEL)),
    ]
    scratch = [pltpu.VMEM((tile + 2 * ATTN_BLOCK, 2 * LANES), BF16),
               pltpu.VMEM((tile + 2 * ATTN_BLOCK, 2 * LANES), BF16),
               pltpu.VMEM((tile, ATTN_DIM), BF16),
               pltpu.VMEM((tile, MLSTM_DIM), BF16)]
    return pl.pallas_call(
        functools.partial(_mix_out_kernel, tile=tile),
        grid=(bsz, n_tiles),
        in_specs=in_specs,
        out_specs=tok(D_MODEL),
        out_shape=jax.ShapeDtypeStruct((bsz, seq, D_MODEL), F32),
        scratch_shapes=scratch + _mlstm_state_scratch(),
        compiler_params=_params(),
        name="mix_out",
    )(lw["sink"], x, pa, gbc, aq, kd, kd, kd, vd, vd, vd, mqkv, smo, grow, hf, attn_bias,
      lw["mnorm_g"], lw["w_b"], lw["w_c"], lw["w_o"])


def _mlp(x, lw, g_final, final_norm):
    bsz, seq, _ = x.shape
    tile = MLP_TILE
    tok = pl.BlockSpec((None, tile, D_MODEL), lambda b, i: (b, i, 0))
    return pl.pallas_call(
        functools.partial(_mlp_kernel, final_norm=final_norm),
        grid=(bsz, seq // tile),
        in_specs=[tok, _resident((1, D_MODEL)), _resident((1, D_MODEL)),
                  _resident((D_MODEL, D_FF)), _resident((D_FF, D_MODEL))],
        out_specs=tok,
        out_shape=jax.ShapeDtypeStruct((bsz, seq, D_MODEL), F32),
        compiler_params=_params(),
        name="mlp",
    )(x, lw["g_mlp"], g_final, lw["w_up"], lw["w_down"])


def _attention_bias():
    q = jnp.arange(ATTN_BLOCK)[:, None]
    k = jnp.arange(3 * ATTN_BLOCK)[None, :]
    dist = jnp.abs(k - ATTN_BLOCK - q).astype(F32)
    slopes = jnp.exp2(-8.0 * (jnp.arange(ATTN_HEADS, dtype=F32) + 1.0) / ATTN_HEADS)
    bias = jnp.where(dist <= WINDOW, -slopes[:, None, None] * dist, -jnp.inf)
    return bias.reshape(ATTN_KV_HEADS, ATTN_GROUP * ATTN_BLOCK, 3 * ATTN_BLOCK)


def _layer_weights(l, w_in, conv_w, attn_sink, mlstm_gate_b, mlstm_norm_g, w_out_a, w_out_b,
                   w_out_c, w_o, norm_mix_g, norm_mlp_g, w_mlp_up, w_mlp_down):
    n_main = C_MERGE
    w = w_in[l]
    w_proj = jnp.concatenate(
        [w[:, :n_main].astype(BF16), w[:, n_main + N_GATES:].astype(BF16),
         w[:, n_main:n_main + N_GATES].astype(BF16), jnp.zeros((D_MODEL, LANES - N_GATES), BF16)],
        axis=1)
    gate_b = jnp.concatenate([mlstm_gate_b[l], jnp.zeros((LANES - N_GATES,), F32)])[None, :]
    return {
        "w_proj": w_proj,
        "conv_w": conv_w[l],
        "sink": attn_sink[l],
        "gate_b": gate_b,
        "mnorm_g": mlstm_norm_g[l][None, :],
        "w_a": w_out_a[l].astype(BF16),
        "w_b": w_out_b[l].astype(BF16),
        "w_c": w_out_c[l].astype(BF16),
        "w_o": w_o[l].astype(BF16),
        "g_mix": norm_mix_g[l][None, :],
        "g_mlp": norm_mlp_g[l][None, :],
        "w_up": w_mlp_up[l].astype(BF16),
        "w_down": w_mlp_down[l].astype(BF16),
    }


def kernel(x_prompt, x_sample, w_in, conv_w, attn_sink, mlstm_gate_b, mlstm_norm_g, w_out_a,
           w_out_b, w_out_c, w_o, norm_mix_g, norm_mlp_g, w_mlp_up, w_mlp_down, norm_final_g):
    depth = w_in.shape[0]
    layers = [_layer_weights(l, w_in, conv_w, attn_sink, mlstm_gate_b, mlstm_norm_g, w_out_a,
                             w_out_b, w_out_c, w_o, norm_mix_g, norm_mlp_g, w_mlp_up, w_mlp_down)
              for l in range(depth)]
    attn_bias = _attention_bias()
    g_final = norm_final_g[None, :]

    def trunk(x):
        for l, lw in enumerate(layers):
            x = _mix_out(x, _mix_in(x, lw), lw, attn_bias)
            x = _mlp(x, lw, g_final, final_norm=(l == depth - 1))
        return x

    return (trunk(x_prompt), trunk(x_sample))
```

```python
import functools

import jax
import jax.numpy as jnp
from jax import lax
from jax.experimental import pallas as pl
from jax.experimental.pallas import tpu as pltpu

F32 = jnp.float32
BF16 = jnp.bfloat16

LANES = 128
SUBLANES = 8
VMEM_LIMIT_BYTES = 56 * 1024 * 1024

D_MODEL = 1024
CONV_DIM = 512
ATTN_HEADS = 8
ATTN_KV_HEADS = 2
ATTN_GROUP = ATTN_HEADS // ATTN_KV_HEADS
ATTN_HEAD_DIM = 64
ATTN_DIM = ATTN_HEADS * ATTN_HEAD_DIM
ATTN_KV_DIM = ATTN_KV_HEADS * ATTN_HEAD_DIM
WINDOW = 128
ATTN_BLOCK = 128
MLSTM_HEADS = 4
MLSTM_HEAD_DIM = 128
MLSTM_DIM = MLSTM_HEADS * MLSTM_HEAD_DIM
MLSTM_CHUNK = 128
M_INIT = -1e30
D_FF = 4 * D_MODEL
RMS_EPS = 1e-6
N_GATES = 4 * MLSTM_HEADS

C_CONV = 0
C_AQ = C_CONV + 3 * CONV_DIM
C_AK = C_AQ + ATTN_DIM
C_AV = C_AK + ATTN_KV_DIM
C_ML = C_AV + ATTN_KV_DIM
C_GA = C_ML + 4 * MLSTM_DIM
C_GATE = C_GA + D_MODEL
N_PROJ = C_GATE + LANES

TILE = 512
MLP_TILE = 1024
FF_CHUNK = 1024
GA_BLOCK = 256
GBC_BLOCK = 256
MLSTM_PIECES_PER_STEP = 4


def _dot(a, b):
    return jnp.dot(a, b, preferred_element_type=F32)


def _dot_nt(a, b):
    return lax.dot_general(a, b, (((1,), (1,)), ((), ())), preferred_element_type=F32)


def _rmsnorm(x, g):
    r = lax.rsqrt(jnp.mean(x * x, axis=-1, keepdims=True) + RMS_EPS)
    return (x * r) * g


def _log_sigmoid(x):
    return jnp.minimum(x, 0.0) - jnp.log(1.0 + jnp.exp(-jnp.abs(x)))


def _mlstm_tile(qkv_ref, gate_rows, st_ref, m_ref, reverse, emit):
    L = MLSTM_CHUNK
    dh = MLSTM_HEAD_DIM
    n_chunks = len(gate_rows)
    order = list(reversed(range(n_chunks))) if reverse else list(range(n_chunks))
    row = lax.broadcasted_iota(jnp.int32, (L, L), 0)
    col = lax.broadcasted_iota(jnp.int32, (L, L), 1)
    keep = (col >= row) if reverse else (col <= row)
    tri = jnp.where((row >= col) if reverse else (row <= col), 1.0, 0.0).astype(F32)
    ones = jnp.ones((L, dh), BF16)

    stats = {}
    m_prev = m_ref[...]
    for c in order:
        gates = gate_rows[c]
        logf = _log_sigmoid(pltpu.roll(gates, MLSTM_HEADS, axis=0))
        b = jnp.dot(logf, tri, precision=lax.Precision.HIGHEST, preferred_element_type=F32)
        r = gates - b
        g = b[:, 0:1] if reverse else b[:, L - 1:L]
        w_state = g + r
        m_loc = jnp.max(w_state, axis=1, keepdims=True)
        m_new = jnp.maximum(g + m_prev, m_loc)
        stats[c] = (logf, r, jnp.exp(w_state - m_loc), m_prev,
                    jnp.exp(g + m_prev - m_new), jnp.exp(m_loc - m_new))
        m_prev = m_new
    m_ref[...] = m_prev
    yield

    work = {}
    for c in order:
        logf, r, e_state, m_in, _, _ = stats[c]
        rows = slice(c * L, (c + 1) * L)
        for h in range(MLSTM_HEADS):
            qh = qkv_ref[rows, h * dh:(h + 1) * dh]
            kh = qkv_ref[rows, MLSTM_DIM + h * dh:MLSTM_DIM + (h + 1) * dh]
            vh = qkv_ref[rows, 2 * MLSTM_DIM + h * dh:2 * MLSTM_DIM + (h + 1) * dh]
            rm = jnp.where(keep, r[h:h + 1, :], -jnp.inf)
            b_t = jnp.sum(jnp.where(keep, logf[h:h + 1, :], 0.0), axis=1, keepdims=True)
            mp = m_in[h:h + 1, 0:1]
            c_t = jnp.maximum(jnp.max(rm, axis=1, keepdims=True), mp)
            p = jnp.exp(rm - c_t) * _dot_nt(qh, kh)
            a_t = jnp.exp(mp - c_t)
            lhs = jnp.concatenate([p.astype(BF16), (a_t * qh.astype(F32)).astype(BF16)], axis=1)
            v1 = jnp.concatenate([vh, ones], axis=1)
            ke_t = (jnp.transpose(kh.astype(F32)) * e_state[h:h + 1, :]).astype(BF16)
            work[c, h] = (lhs, v1, jnp.exp(-(b_t + c_t)), _dot(ke_t, v1))
            yield

    state = [st_ref[h] for h in range(MLSTM_HEADS)]
    for c in order:
        _, _, _, _, a_vec, c_vec = stats[c]
        for h in range(MLSTM_HEADS):
            lhs, v1, bound, upd = work[c, h]
            tot = _dot(lhs, jnp.concatenate([v1, state[h].astype(BF16)], axis=0))
            emit(c, h, tot[:, :dh] / jnp.maximum(jnp.abs(tot[:, dh:]), bound))
            state[h] = a_vec[h:h + 1, 0:1] * state[h] + c_vec[h:h + 1, 0:1] * upd
            yield
    for h in range(MLSTM_HEADS):
        st_ref[h] = state[h]


def _init_state(st_ref, m_ref):
    st_ref[...] = jnp.zeros(st_ref.shape, F32)
    m_ref[...] = jnp.full(m_ref.shape, M_INIT, F32)


def _mix_in_kernel(x_ref, xp_ref, xn_ref, gmix_ref, w_ref, convw_ref, gateb_ref, wa_ref,
                   pa_ref, aq_ref, kd_ref, vd_ref, mqkv_ref, smo_ref, grow_ref, hf_ref,
                   st_scr, m_scr, *, tile):
    i = pl.program_id(1)
    n_tiles = pl.num_programs(1)

    @pl.when(i == 0)
    def _():
        _init_state(st_scr, m_scr)

    g_mix = gmix_ref[...]
    h = _rmsnorm(x_ref[...], g_mix).astype(BF16)

    ul = _dot(h, w_ref[:, C_ML:C_ML + 4 * MLSTM_DIM])
    mqkv_ref[:, :MLSTM_DIM] = ul[:, :MLSTM_DIM].astype(BF16)
    mqkv_ref[:, MLSTM_DIM:2 * MLSTM_DIM] = (
        ul[:, MLSTM_DIM:2 * MLSTM_DIM] * (MLSTM_HEAD_DIM ** -0.5)).astype(BF16)
    mqkv_ref[:, 2 * MLSTM_DIM:] = ul[:, 2 * MLSTM_DIM:3 * MLSTM_DIM].astype(BF16)
    smo_ref[...] = jax.nn.sigmoid(ul[:, 3 * MLSTM_DIM:]).astype(BF16)
    gcol = _dot(h, w_ref[:, C_GATE:C_GATE + LANES]) + gateb_ref[...]
    gate_rows = []
    for c in range(tile // MLSTM_CHUNK):
        rows = slice(c * MLSTM_CHUNK, (c + 1) * MLSTM_CHUNK)
        g_rows = jnp.transpose(gcol[rows, :])[:2 * SUBLANES, :]
        grow_ref[:, rows] = g_rows
        gate_rows.append(g_rows[:SUBLANES, :])

    def emit_fwd(c, hd, out):
        hf_ref[c * MLSTM_CHUNK:(c + 1) * MLSTM_CHUNK, hd * MLSTM_HEAD_DIM:(hd + 1) * MLSTM_HEAD_DIM] = out

    mlstm = _mlstm_tile(mqkv_ref, gate_rows, st_scr, m_scr, False, emit_fwd)

    def conv_proj():
        halo = jnp.concatenate([_rmsnorm(xp_ref[...], g_mix), _rmsnorm(xn_ref[...], g_mix)], axis=0)
        uh = _dot(halo.astype(BF16), w_ref[:, C_CONV + CONV_DIM:C_CONV + 3 * CONV_DIM])
        zh = uh[:, :CONV_DIM] * uh[:, CONV_DIM:]
        z_prev = jnp.where(i > 0, zh[SUBLANES - 1:SUBLANES, :], 0.0)
        z_next = jnp.where(i < n_tiles - 1, zh[SUBLANES:SUBLANES + 1, :], 0.0)
        return _dot(h, w_ref[:, C_CONV:C_CONV + 3 * CONV_DIM]), z_prev, z_next

    def conv_out(uc, z_prev, z_next):
        z = uc[:, CONV_DIM:2 * CONV_DIM] * uc[:, 2 * CONV_DIM:]
        trow = lax.broadcasted_iota(jnp.int32, (tile, CONV_DIM), 0)
        z_dn = jnp.where(trow == 0, z_prev, pltpu.roll(z, 1, axis=0))
        z_up = jnp.where(trow == tile - 1, z_next, pltpu.roll(z, tile - 1, axis=0))
        conv = convw_ref[0:1, :] * z_dn + convw_ref[1:2, :] * z + convw_ref[2:3, :] * z_up
        return _dot((uc[:, :CONV_DIM] * conv).astype(BF16), wa_ref[...])

    def ga_block(y_a, blk):
        cols = slice(blk * GA_BLOCK, (blk + 1) * GA_BLOCK)
        sg = jax.nn.sigmoid(_dot(h, w_ref[:, C_GA + blk * GA_BLOCK:C_GA + (blk + 1) * GA_BLOCK]))
        pa_ref[:, cols] = (sg * y_a[:, cols]).astype(BF16)

    def attn_q():
        aq_ref[...] = (_dot(h, w_ref[:, C_AQ:C_AK]) * (ATTN_HEAD_DIM ** -0.5)).astype(BF16)

    def attn_kv():
        ukv = _dot(h, w_ref[:, C_AK:C_AV + ATTN_KV_DIM])
        lane = lax.broadcasted_iota(jnp.int32, (tile, LANES), 1)
        low = lane < ATTN_HEAD_DIM
        for half, dst in ((ukv[:, :ATTN_KV_DIM], kd_ref), (ukv[:, ATTN_KV_DIM:], vd_ref)):
            swapped = pltpu.roll(half, ATTN_HEAD_DIM, axis=1)
            dst[:, :LANES] = jnp.where(low, half, swapped).astype(BF16)
            dst[:, LANES:] = jnp.where(low, swapped, half).astype(BF16)

    def advance():
        for _ in range(MLSTM_PIECES_PER_STEP):
            next(mlstm, None)

    advance()
    conv_parts = conv_proj()
    advance()
    y_a = conv_out(*conv_parts)
    advance()
    attn_kv()
    advance()
    attn_q()
    advance()
    n_ga = D_MODEL // GA_BLOCK
    for blk in range(n_ga - 1):
        ga_block(y_a, blk)
        advance()
    for _ in mlstm:
        pass
    ga_block(y_a, n_ga - 1)


def _mix_out_kernel(sink_ref, x_ref, gmix_ref, pa_ref, aq_ref, kd_ref, kdp_ref, kdn_ref,
                    vd_ref, vdp_ref, vdn_ref, mqkv_ref, smo_ref, grow_ref, hf_ref, bias_ref,
                    mnorm_ref, wg_ref, wb_ref, wc_ref, wo_ref,
                    xo_ref,
                    kwin, vwin, att_scr, yc_scr, sg_scr, st_scr, m_scr, *, tile):
    i = pl.program_id(1)
    n_tiles = pl.num_programs(1)
    ti = n_tiles - 1 - i
    n_qblk = tile // ATTN_BLOCK

    @pl.when(i == 0)
    def _():
        _init_state(st_scr, m_scr)

    kwin[0:ATTN_BLOCK, :] = kdp_ref[...]
    kwin[ATTN_BLOCK:ATTN_BLOCK + tile, :] = kd_ref[...]
    kwin[ATTN_BLOCK + tile:, :] = kdn_ref[...]
    vwin[0:ATTN_BLOCK, :] = vdp_ref[...]
    vwin[ATTN_BLOCK:ATTN_BLOCK + tile, :] = vd_ref[...]
    vwin[ATTN_BLOCK + tile:, :] = vdn_ref[...]

    h = _rmsnorm(x_ref[...], gmix_ref[...]).astype(BF16)
    lane = lax.broadcasted_iota(jnp.int32, (ATTN_BLOCK, LANES), 1)
    low = lane < ATTN_HEAD_DIM
    kcol = lax.broadcasted_iota(jnp.int32, (1, 3 * ATTN_BLOCK), 1)

    def gate_block(blk):
        cols = slice(blk * GBC_BLOCK, (blk + 1) * GBC_BLOCK)
        sg_scr[:, cols] = jax.nn.sigmoid(_dot(h, wg_ref[:, cols])).astype(BF16)

    def attn_unit(j, g):
        rows = slice(j * ATTN_BLOCK, (j + 1) * ATTN_BLOCK)
        wrows = slice(j * ATTN_BLOCK, (j + 3) * ATTN_BLOCK)
        outside = None
        if j == 0:
            outside = jnp.logical_and(kcol < ATTN_BLOCK, ti == 0)
        if j == n_qblk - 1:
            after = jnp.logical_and(kcol >= 2 * ATTN_BLOCK, ti == n_tiles - 1)
            outside = after if outside is None else jnp.logical_or(outside, after)
        kw = kwin[wrows, g * LANES:(g + 1) * LANES]
        vw = vwin[wrows, g * LANES:(g + 1) * LANES]
        qs = []
        for pr in range(2):
            pair = g * 2 + pr
            qp = aq_ref[rows, pair * LANES:(pair + 1) * LANES]
            qs.append(jnp.where(low, qp, jnp.zeros_like(qp)))
            qs.append(jnp.where(low, jnp.zeros_like(qp), qp))
        bias = bias_ref[g]
        if outside is not None:
            bias = bias + jnp.where(outside, -jnp.inf, 0.0)
        s_all = _dot_nt(jnp.concatenate(qs, axis=0), kw) + bias
        gate_block(j * ATTN_KV_HEADS + g)
        outs = []
        for idx in range(ATTN_GROUP):
            sk = sink_ref[g * ATTN_GROUP + idx]
            s = s_all[idx * ATTN_BLOCK:(idx + 1) * ATTN_BLOCK, :]
            m = jnp.maximum(jnp.max(s, axis=1, keepdims=True), sk)
            p = jnp.exp(s - m)
            denom = jnp.sum(p, axis=1, keepdims=True) + jnp.exp(sk - m)
            outs.append(_dot(p.astype(BF16), vw) / denom)
        for pr in range(2):
            pair = g * 2 + pr
            att_scr[rows, pair * LANES:(pair + 1) * LANES] = jnp.where(
                low, outs[2 * pr], outs[2 * pr + 1]).astype(BF16)

    def emit_bwd(c, hd, out):
        rows = slice(c * MLSTM_CHUNK, (c + 1) * MLSTM_CHUNK)
        sl = slice(hd * MLSTM_HEAD_DIM, (hd + 1) * MLSTM_HEAD_DIM)
        hm = hf_ref[rows, sl] + out
        hm = hm * lax.rsqrt(jnp.mean(hm * hm, axis=-1, keepdims=True) + RMS_EPS)
        yc_scr[rows, sl] = (smo_ref[rows, sl].astype(F32) * (hm * mnorm_ref[:, sl])).astype(BF16)

    gate_rows = [grow_ref[SUBLANES:, c * MLSTM_CHUNK:(c + 1) * MLSTM_CHUNK]
                 for c in range(tile // MLSTM_CHUNK)]
    mlstm = _mlstm_tile(mqkv_ref, gate_rows, st_scr, m_scr, True, emit_bwd)

    assert n_qblk * ATTN_KV_HEADS * GBC_BLOCK == 2 * D_MODEL
    for j in range(n_qblk):
        for g in range(ATTN_KV_HEADS):
            attn_unit(j, g)
            for _ in range(MLSTM_PIECES_PER_STEP):
                next(mlstm, None)
    for _ in mlstm:
        pass

    y_b = _dot(att_scr[...], wb_ref[...]).astype(BF16)
    y_c = _dot(yc_scr[...], wc_ref[...]).astype(BF16)
    merged = pa_ref[...] + sg_scr[:, :D_MODEL] * y_b + sg_scr[:, D_MODEL:] * y_c
    xo_ref[...] = x_ref[...] + _dot(merged, wo_ref[...])


def _mlp_kernel(x_ref, g_ref, gf_ref, wup_ref, wdn_ref, o_ref, *, final_norm):
    x = x_ref[...]
    h = _rmsnorm(x, g_ref[...]).astype(BF16)
    acc = x
    for c in range(D_FF // FF_CHUNK):
        cols = slice(c * FF_CHUNK, (c + 1) * FF_CHUNK)
        up = jnp.maximum(_dot(h, wup_ref[:, cols]), 0.0)
        acc = acc + _dot((up * up).astype(BF16), wdn_ref[cols, :])
    if final_norm:
        acc = _rmsnorm(acc, gf_ref[...])
    o_ref[...] = acc


def _resident(shape):
    nd = len(shape)
    return pl.BlockSpec(shape, lambda b, i: (0,) * nd, pipeline_mode=pl.Buffered(1))


def _params():
    return pltpu.CompilerParams(dimension_semantics=("arbitrary", "arbitrary"),
                                vmem_limit_bytes=VMEM_LIMIT_BYTES)


def _mlstm_state_scratch():
    return [pltpu.VMEM((MLSTM_HEADS, MLSTM_HEAD_DIM, 2 * MLSTM_HEAD_DIM), F32),
            pltpu.VMEM((SUBLANES, LANES), F32)]


def _mix_in(x, lw):
    bsz, seq, _ = x.shape
    tile = TILE
    n_tiles = seq // tile
    halo_per_tile = tile // SUBLANES
    n_halo = seq // SUBLANES

    def tok(width):
        return pl.BlockSpec((None, tile, width), lambda b, i: (b, i, 0))

    in_specs = [
        tok(D_MODEL),
        pl.BlockSpec((None, SUBLANES, D_MODEL),
                     lambda b, i: (b, jnp.maximum(i * halo_per_tile - 1, 0), 0)),
        pl.BlockSpec((None, SUBLANES, D_MODEL),
                     lambda b, i: (b, jnp.minimum((i + 1) * halo_per_tile, n_halo - 1), 0)),
        _resident((1, D_MODEL)),
        _resident((D_MODEL, N_PROJ)),
        _resident((3, CONV_DIM)),
        _resident((1, LANES)),
        _resident((CONV_DIM, D_MODEL)),
    ]
    widths = (D_MODEL, ATTN_DIM, 2 * LANES, 2 * LANES, 3 * MLSTM_DIM, MLSTM_DIM)
    out_shape = [jax.ShapeDtypeStruct((bsz, seq, w), BF16) for w in widths]
    out_shape += [jax.ShapeDtypeStruct((bsz, 2 * SUBLANES, seq), F32),
                  jax.ShapeDtypeStruct((bsz, seq, MLSTM_DIM), F32)]
    out_specs = [tok(w) for w in widths]
    out_specs += [pl.BlockSpec((None, 2 * SUBLANES, tile), lambda b, i: (b, 0, i)), tok(MLSTM_DIM)]
    return pl.pallas_call(
        functools.partial(_mix_in_kernel, tile=tile),
        grid=(bsz, n_tiles),
        in_specs=in_specs,
        out_specs=out_specs,
        out_shape=out_shape,
        scratch_shapes=_mlstm_state_scratch(),
        compiler_params=_params(),
        name="mix_in",
    )(x, x, x, lw["g_mix"], lw["w_proj"], lw["conv_w"], lw["gate_b"], lw["w_a"])


def _mix_out(x, mid, lw, attn_bias):
    pa, aq, kd, vd, mqkv, smo, grow, hf = mid
    bsz, seq, _ = x.shape
    tile = TILE
    n_tiles = seq // tile
    blk_per_tile = tile // ATTN_BLOCK
    n_blk = seq // ATTN_BLOCK

    def tok(width):
        return pl.BlockSpec((None, tile, width), lambda b, i: (b, n_tiles - 1 - i, 0))

    def blk_prev(width):
        return pl.BlockSpec(
            (None, ATTN_BLOCK, width),
            lambda b, i: (b, jnp.maximum((n_tiles - 1 - i) * blk_per_tile - 1, 0), 0))

    def blk_next(width):
        return pl.BlockSpec(
            (None, ATTN_BLOCK, width),
            lambda b, i: (b, jnp.minimum((n_tiles - i) * blk_per_tile, n_blk - 1), 0))

    in_specs = [
        pl.BlockSpec(memory_space=pltpu.SMEM),
        tok(D_MODEL), _resident((1, D_MODEL)), tok(D_MODEL), tok(ATTN_DIM),
        tok(2 * LANES), blk_prev(2 * LANES), blk_next(2 * LANES),
        tok(2 * LANES), blk_prev(2 * LANES), blk_next(2 * LANES),
        tok(3 * MLSTM_DIM), tok(MLSTM_DIM),
        pl.BlockSpec((None, 2 * SUBLANES, tile), lambda b, i: (b, 0, n_tiles - 1 - i)),
        tok(MLSTM_DIM),
        _resident((ATTN_KV_HEADS, ATTN_GROUP * ATTN_BLOCK, 3 * ATTN_BLOCK)),
        _resident((1, MLSTM_DIM)),
        _resident((D_MODEL, 2 * D_MODEL)),
        _resident((ATTN_DIM, D_MODEL)),
        _resident((MLSTM_DIM, D_MODEL)),
        _resident((D_MODEL, D_MODEL)),
    ]
    scratch = [pltpu.VMEM((tile + 2 * ATTN_BLOCK, 2 * LANES), BF16),
               pltpu.VMEM((tile + 2 * ATTN_BLOCK, 2 * LANES), BF16),
               pltpu.VMEM((tile, ATTN_DIM), BF16),
               pltpu.VMEM((tile, MLSTM_DIM), BF16),
               pltpu.VMEM((tile, 2 * D_MODEL), BF16)]
    return pl.pallas_call(
        functools.partial(_mix_out_kernel, tile=tile),
        grid=(bsz, n_tiles),
        in_specs=in_specs,
        out_specs=tok(D_MODEL),
        out_shape=jax.ShapeDtypeStruct((bsz, seq, D_MODEL), F32),
        scratch_shapes=scratch + _mlstm_state_scratch(),
        compiler_params=_params(),
        name="mix_out",
    )(lw["sink"], x, lw["g_mix"], pa, aq, kd, kd, kd, vd, vd, vd, mqkv, smo, grow, hf, attn_bias,
      lw["mnorm_g"], lw["w_gbc"], lw["w_b"], lw["w_c"], lw["w_o"])


def _mlp(x, lw, g_final, final_norm):
    bsz, seq, _ = x.shape
    tile = MLP_TILE
    tok = pl.BlockSpec((None, tile, D_MODEL), lambda b, i: (b, i, 0))
    return pl.pallas_call(
        functools.partial(_mlp_kernel, final_norm=final_norm),
        grid=(bsz, seq // tile),
        in_specs=[tok, _resident((1, D_MODEL)), _resident((1, D_MODEL)),
                  _resident((D_MODEL, D_FF)), _resident((D_FF, D_MODEL))],
        out_specs=tok,
        out_shape=jax.ShapeDtypeStruct((bsz, seq, D_MODEL), F32),
        compiler_params=_params(),
        name="mlp",
    )(x, lw["g_mlp"], g_final, lw["w_up"], lw["w_down"])


def _attention_bias():
    q = jnp.arange(ATTN_BLOCK)[:, None]
    k = jnp.arange(3 * ATTN_BLOCK)[None, :]
    dist = jnp.abs(k - ATTN_BLOCK - q).astype(F32)
    slopes = jnp.exp2(-8.0 * (jnp.arange(ATTN_HEADS, dtype=F32) + 1.0) / ATTN_HEADS)
    bias = jnp.where(dist <= WINDOW, -slopes[:, None, None] * dist, -jnp.inf)
    return bias.reshape(ATTN_KV_HEADS, ATTN_GROUP * ATTN_BLOCK, 3 * ATTN_BLOCK)


def _layer_weights(l, w_in, conv_w, attn_sink, mlstm_gate_b, mlstm_norm_g, w_out_a, w_out_b,
                   w_out_c, w_o, norm_mix_g, norm_mlp_g, w_mlp_up, w_mlp_down):
    n_main = C_GA
    w = w_in[l]
    w_proj = jnp.concatenate(
        [w[:, :n_main].astype(BF16), w[:, n_main + N_GATES:n_main + N_GATES + D_MODEL].astype(BF16),
         w[:, n_main:n_main + N_GATES].astype(BF16), jnp.zeros((D_MODEL, LANES - N_GATES), BF16)],
        axis=1)
    gate_b = jnp.concatenate([mlstm_gate_b[l], jnp.zeros((LANES - N_GATES,), F32)])[None, :]
    return {
        "w_proj": w_proj,
        "w_gbc": w[:, n_main + N_GATES + D_MODEL:].astype(BF16),
        "conv_w": conv_w[l],
        "sink": attn_sink[l],
        "gate_b": gate_b,
        "mnorm_g": mlstm_norm_g[l][None, :],
        "w_a": w_out_a[l].astype(BF16),
        "w_b": w_out_b[l].astype(BF16),
        "w_c": w_out_c[l].astype(BF16),
        "w_o": w_o[l].astype(BF16),
        "g_mix": norm_mix_g[l][None, :],
        "g_mlp": norm_mlp_g[l][None, :],
        "w_up": w_mlp_up[l].astype(BF16),
        "w_down": w_mlp_down[l].astype(BF16),
    }


def kernel(x_prompt, x_sample, w_in, conv_w, attn_sink, mlstm_gate_b, mlstm_norm_g, w_out_a,
           w_out_b, w_out_c, w_o, norm_mix_g, norm_mlp_g, w_mlp_up, w_mlp_down, norm_final_g):
    depth = w_in.shape[0]
    layers = [_layer_weights(l, w_in, conv_w, attn_sink, mlstm_gate_b, mlstm_norm_g, w_out_a,
                             w_out_b, w_out_c, w_o, norm_mix_g, norm_mlp_g, w_mlp_up, w_mlp_down)
              for l in range(depth)]
    attn_bias = _attention_bias()
    g_final = norm_final_g[None, :]

    def trunk(x):
        for l, lw in enumerate(layers):
            x = _mix_out(x, _mix_in(x, lw), lw, attn_bias)
            x = _mlp(x, lw, g_final, final_norm=(l == depth - 1))
        return x

    return (trunk(x_prompt), trunk(x_sample))
```

```python
import functools

import jax
import jax.numpy as jnp
from jax import lax
from jax.experimental import pallas as pl
from jax.experimental.pallas import tpu as pltpu

F32 = jnp.float32
BF16 = jnp.bfloat16

LANES = 128
SUBLANES = 8
VMEM_LIMIT_BYTES = 56 * 1024 * 1024

D_MODEL = 1024
CONV_DIM = 512
ATTN_HEADS = 8
ATTN_KV_HEADS = 2
ATTN_GROUP = ATTN_HEADS // ATTN_KV_HEADS
ATTN_HEAD_DIM = 64
ATTN_DIM = ATTN_HEADS * ATTN_HEAD_DIM
ATTN_KV_DIM = ATTN_KV_HEADS * ATTN_HEAD_DIM
WINDOW = 128
ATTN_BLOCK = 128
MLSTM_HEADS = 4
MLSTM_HEAD_DIM = 128
MLSTM_DIM = MLSTM_HEADS * MLSTM_HEAD_DIM
MLSTM_CHUNK = 128
M_INIT = -1e30
D_FF = 4 * D_MODEL
RMS_EPS = 1e-6
N_GATES = 4 * MLSTM_HEADS

C_CONV = 0
C_AQ = C_CONV + 3 * CONV_DIM
C_AK = C_AQ + ATTN_DIM
C_AV = C_AK + ATTN_KV_DIM
C_ML = C_AV + ATTN_KV_DIM
C_GA = C_ML + 4 * MLSTM_DIM
C_GATE = C_GA + D_MODEL
N_PROJ = C_GATE + LANES

TILE = 512
MLP_TILE = 1024
FF_CHUNK = 1024
GA_BLOCK = 256
MLSTM_PIECES_PER_PROJ = 5
MLSTM_PIECES_PER_ATTN = 4


def _dot(a, b):
    return jnp.dot(a, b, preferred_element_type=F32)


def _dot_nt(a, b):
    return lax.dot_general(a, b, (((1,), (1,)), ((), ())), preferred_element_type=F32)


def _rmsnorm(x, g):
    r = lax.rsqrt(jnp.mean(x * x, axis=-1, keepdims=True) + RMS_EPS)
    return (x * r) * g


def _log_sigmoid(x):
    return jnp.minimum(x, 0.0) - jnp.log(1.0 + jnp.exp(-jnp.abs(x)))


def _mlstm_tile(qkv_ref, gate_rows, st_ref, m_ref, reverse, emit):
    L = MLSTM_CHUNK
    dh = MLSTM_HEAD_DIM
    n_chunks = len(gate_rows)
    order = list(reversed(range(n_chunks))) if reverse else list(range(n_chunks))
    row = lax.broadcasted_iota(jnp.int32, (L, L), 0)
    col = lax.broadcasted_iota(jnp.int32, (L, L), 1)
    keep = (col >= row) if reverse else (col <= row)
    tri = jnp.where((row >= col) if reverse else (row <= col), 1.0, 0.0).astype(F32)
    ones = jnp.ones((L, dh), BF16)

    stats = {}
    m_prev = m_ref[...]
    for c in order:
        gates = gate_rows[c]
        logf = _log_sigmoid(pltpu.roll(gates, MLSTM_HEADS, axis=0))
        b = jnp.dot(logf, tri, precision=lax.Precision.HIGHEST, preferred_element_type=F32)
        r = gates - b
        g = b[:, 0:1] if reverse else b[:, L - 1:L]
        w_state = g + r
        m_loc = jnp.max(w_state, axis=1, keepdims=True)
        m_new = jnp.maximum(g + m_prev, m_loc)
        stats[c] = (logf, r, jnp.exp(w_state - m_loc), m_prev,
                    jnp.exp(g + m_prev - m_new), jnp.exp(m_loc - m_new))
        m_prev = m_new
    m_ref[...] = m_prev
    yield

    work = {}
    for c in order:
        logf, r, e_state, m_in, _, _ = stats[c]
        rows = slice(c * L, (c + 1) * L)
        for h in range(MLSTM_HEADS):
            qh = qkv_ref[rows, h * dh:(h + 1) * dh]
            kh = qkv_ref[rows, MLSTM_DIM + h * dh:MLSTM_DIM + (h + 1) * dh]
            vh = qkv_ref[rows, 2 * MLSTM_DIM + h * dh:2 * MLSTM_DIM + (h + 1) * dh]
            rm = jnp.where(keep, r[h:h + 1, :], -jnp.inf)
            b_t = jnp.sum(jnp.where(keep, logf[h:h + 1, :], 0.0), axis=1, keepdims=True)
            mp = m_in[h:h + 1, 0:1]
            c_t = jnp.maximum(jnp.max(rm, axis=1, keepdims=True), mp)
            p = jnp.exp(rm - c_t) * _dot_nt(qh, kh)
            a_t = jnp.exp(mp - c_t)
            lhs = jnp.concatenate([p.astype(BF16), (a_t * qh.astype(F32)).astype(BF16)], axis=1)
            v1 = jnp.concatenate([vh, ones], axis=1)
            ke_t = (jnp.transpose(kh.astype(F32)) * e_state[h:h + 1, :]).astype(BF16)
            work[c, h] = (lhs, v1, jnp.exp(-(b_t + c_t)), _dot(ke_t, v1))
            yield

    state = [st_ref[h] for h in range(MLSTM_HEADS)]
    for c in order:
        _, _, _, _, a_vec, c_vec = stats[c]
        for h in range(MLSTM_HEADS):
            lhs, v1, bound, upd = work[c, h]
            tot = _dot(lhs, jnp.concatenate([v1, state[h].astype(BF16)], axis=0))
            emit(c, h, tot[:, :dh] / jnp.maximum(jnp.abs(tot[:, dh:]), bound))
            state[h] = a_vec[h:h + 1, 0:1] * state[h] + c_vec[h:h + 1, 0:1] * upd
            yield
    for h in range(MLSTM_HEADS):
        st_ref[h] = state[h]


def _init_state(st_ref, m_ref):
    st_ref[...] = jnp.zeros(st_ref.shape, F32)
    m_ref[...] = jnp.full(m_ref.shape, M_INIT, F32)


def _mix_in_kernel(x_ref, xp_ref, xn_ref, gmix_ref, w_ref, convw_ref, gateb_ref, wa_ref,
                   pa_ref, aq_ref, kd_ref, vd_ref, mqkv_ref, smo_ref, grow_ref, hf_ref,
                   st_scr, m_scr, *, tile):
    i = pl.program_id(1)
    n_tiles = pl.num_programs(1)

    @pl.when(i == 0)
    def _():
        _init_state(st_scr, m_scr)

    g_mix = gmix_ref[...]
    h = _rmsnorm(x_ref[...], g_mix).astype(BF16)

    ul = _dot(h, w_ref[:, C_ML:C_ML + 4 * MLSTM_DIM])
    gcol = _dot(h, w_ref[:, C_GATE:C_GATE + LANES]) + gateb_ref[...]
    uc = _dot(h, w_ref[:, C_CONV:C_CONV + 3 * CONV_DIM])

    mqkv_ref[:, :MLSTM_DIM] = ul[:, :MLSTM_DIM].astype(BF16)
    mqkv_ref[:, MLSTM_DIM:2 * MLSTM_DIM] = (
        ul[:, MLSTM_DIM:2 * MLSTM_DIM] * (MLSTM_HEAD_DIM ** -0.5)).astype(BF16)
    mqkv_ref[:, 2 * MLSTM_DIM:] = ul[:, 2 * MLSTM_DIM:3 * MLSTM_DIM].astype(BF16)
    smo_ref[...] = jax.nn.sigmoid(ul[:, 3 * MLSTM_DIM:]).astype(BF16)
    gate_rows = []
    for c in range(tile // MLSTM_CHUNK):
        rows = slice(c * MLSTM_CHUNK, (c + 1) * MLSTM_CHUNK)
        g_rows = jnp.transpose(gcol[rows, :])[:2 * SUBLANES, :]
        grow_ref[:, rows] = g_rows
        gate_rows.append(g_rows[:SUBLANES, :])

    def emit_fwd(c, hd, out):
        hf_ref[c * MLSTM_CHUNK:(c + 1) * MLSTM_CHUNK, hd * MLSTM_HEAD_DIM:(hd + 1) * MLSTM_HEAD_DIM] = out

    mlstm = _mlstm_tile(mqkv_ref, gate_rows, st_scr, m_scr, False, emit_fwd)

    def conv_out():
        halo = jnp.concatenate([_rmsnorm(xp_ref[...], g_mix), _rmsnorm(xn_ref[...], g_mix)], axis=0)
        uh = _dot(halo.astype(BF16), w_ref[:, C_CONV + CONV_DIM:C_CONV + 3 * CONV_DIM])
        zh = uh[:, :CONV_DIM] * uh[:, CONV_DIM:]
        z_prev = jnp.where(i > 0, zh[SUBLANES - 1:SUBLANES, :], 0.0)
        z_next = jnp.where(i < n_tiles - 1, zh[SUBLANES:SUBLANES + 1, :], 0.0)
        z = uc[:, CONV_DIM:2 * CONV_DIM] * uc[:, 2 * CONV_DIM:]
        trow = lax.broadcasted_iota(jnp.int32, (tile, CONV_DIM), 0)
        z_dn = jnp.where(trow == 0, z_prev, pltpu.roll(z, 1, axis=0))
        z_up = jnp.where(trow == tile - 1, z_next, pltpu.roll(z, tile - 1, axis=0))
        conv = convw_ref[0:1, :] * z_dn + convw_ref[1:2, :] * z + convw_ref[2:3, :] * z_up
        return _dot((uc[:, :CONV_DIM] * conv).astype(BF16), wa_ref[...])

    def ga_block(y_a, blk):
        cols = slice(blk * GA_BLOCK, (blk + 1) * GA_BLOCK)
        sg = jax.nn.sigmoid(_dot(h, w_ref[:, C_GA + blk * GA_BLOCK:C_GA + (blk + 1) * GA_BLOCK]))
        pa_ref[:, cols] = (sg * y_a[:, cols]).astype(BF16)

    def attn_q():
        aq_ref[...] = (_dot(h, w_ref[:, C_AQ:C_AK]) * (ATTN_HEAD_DIM ** -0.5)).astype(BF16)

    def attn_kv():
        ukv = _dot(h, w_ref[:, C_AK:C_AV + ATTN_KV_DIM])
        lane = lax.broadcasted_iota(jnp.int32, (tile, LANES), 1)
        low = lane < ATTN_HEAD_DIM
        for half, dst in ((ukv[:, :ATTN_KV_DIM], kd_ref), (ukv[:, ATTN_KV_DIM:], vd_ref)):
            swapped = pltpu.roll(half, ATTN_HEAD_DIM, axis=1)
            dst[:, :LANES] = jnp.where(low, half, swapped).astype(BF16)
            dst[:, LANES:] = jnp.where(low, swapped, half).astype(BF16)

    def advance():
        for _ in range(MLSTM_PIECES_PER_PROJ):
            next(mlstm, None)

    advance()
    y_a = conv_out()
    advance()
    n_ga = D_MODEL // GA_BLOCK
    for blk in range(n_ga):
        ga_block(y_a, blk)
        advance()
    attn_kv()
    for _ in mlstm:
        pass
    attn_q()


def _mix_out_kernel(sink_ref, x_ref, gmix_ref, pa_ref, aq_ref, kd_ref, kdp_ref, kdn_ref,
                    vd_ref, vdp_ref, vdn_ref, mqkv_ref, smo_ref, grow_ref, hf_ref, bias_ref,
                    mnorm_ref, wg_ref, wb_ref, wc_ref, wo_ref,
                    xo_ref,
                    kwin, vwin, att_scr, yc_scr, sg_scr, st_scr, m_scr, *, tile):
    i = pl.program_id(1)
    n_tiles = pl.num_programs(1)
    ti = n_tiles - 1 - i
    n_qblk = tile // ATTN_BLOCK

    @pl.when(i == 0)
    def _():
        _init_state(st_scr, m_scr)

    kwin[0:ATTN_BLOCK, :] = kdp_ref[...]
    kwin[ATTN_BLOCK:ATTN_BLOCK + tile, :] = kd_ref[...]
    kwin[ATTN_BLOCK + tile:, :] = kdn_ref[...]
    vwin[0:ATTN_BLOCK, :] = vdp_ref[...]
    vwin[ATTN_BLOCK:ATTN_BLOCK + tile, :] = vd_ref[...]
    vwin[ATTN_BLOCK + tile:, :] = vdn_ref[...]

    h = _rmsnorm(x_ref[...], gmix_ref[...]).astype(BF16)
    lane = lax.broadcasted_iota(jnp.int32, (ATTN_BLOCK, LANES), 1)
    low = lane < ATTN_HEAD_DIM
    kcol = lax.broadcasted_iota(jnp.int32, (1, 3 * ATTN_BLOCK), 1)

    gate_cols = 2 * D_MODEL // (n_qblk * ATTN_KV_HEADS)

    def gate_block(blk):
        cols = slice(blk * gate_cols, (blk + 1) * gate_cols)
        sg_scr[:, cols] = jax.nn.sigmoid(_dot(h, wg_ref[:, cols])).astype(BF16)

    def attn_unit(j, g):
        rows = slice(j * ATTN_BLOCK, (j + 1) * ATTN_BLOCK)
        wrows = slice(j * ATTN_BLOCK, (j + 3) * ATTN_BLOCK)
        outside = None
        if j == 0:
            outside = jnp.logical_and(kcol < ATTN_BLOCK, ti == 0)
        if j == n_qblk - 1:
            after = jnp.logical_and(kcol >= 2 * ATTN_BLOCK, ti == n_tiles - 1)
            outside = after if outside is None else jnp.logical_or(outside, after)
        kw = kwin[wrows, g * LANES:(g + 1) * LANES]
        vw = vwin[wrows, g * LANES:(g + 1) * LANES]
        qs = []
        for pr in range(2):
            pair = g * 2 + pr
            qp = aq_ref[rows, pair * LANES:(pair + 1) * LANES]
            qs.append(jnp.where(low, qp, jnp.zeros_like(qp)))
            qs.append(jnp.where(low, jnp.zeros_like(qp), qp))
        bias = bias_ref[g]
        if outside is not None:
            bias = bias + jnp.where(outside, -jnp.inf, 0.0)
        s_all = _dot_nt(jnp.concatenate(qs, axis=0), kw) + bias
        gate_block(j * ATTN_KV_HEADS + g)
        outs = []
        for idx in range(ATTN_GROUP):
            sk = sink_ref[g * ATTN_GROUP + idx]
            s = s_all[idx * ATTN_BLOCK:(idx + 1) * ATTN_BLOCK, :]
            m = jnp.maximum(jnp.max(s, axis=1, keepdims=True), sk)
            p = jnp.exp(s - m)
            denom = jnp.sum(p, axis=1, keepdims=True) + jnp.exp(sk - m)
            outs.append(_dot(p.astype(BF16), vw) / denom)
        for pr in range(2):
            pair = g * 2 + pr
            att_scr[rows, pair * LANES:(pair + 1) * LANES] = jnp.where(
                low, outs[2 * pr], outs[2 * pr + 1]).astype(BF16)

    def emit_bwd(c, hd, out):
        rows = slice(c * MLSTM_CHUNK, (c + 1) * MLSTM_CHUNK)
        sl = slice(hd * MLSTM_HEAD_DIM, (hd + 1) * MLSTM_HEAD_DIM)
        hm = hf_ref[rows, sl] + out
        hm = hm * lax.rsqrt(jnp.mean(hm * hm, axis=-1, keepdims=True) + RMS_EPS)
        yc_scr[rows, sl] = (smo_ref[rows, sl].astype(F32) * (hm * mnorm_ref[:, sl])).astype(BF16)

    gate_rows = [grow_ref[SUBLANES:, c * MLSTM_CHUNK:(c + 1) * MLSTM_CHUNK]
                 for c in range(tile // MLSTM_CHUNK)]
    mlstm = _mlstm_tile(mqkv_ref, gate_rows, st_scr, m_scr, True, emit_bwd)

    for j in range(n_qblk):
        for g in range(ATTN_KV_HEADS):
            attn_unit(j, g)
            for _ in range(MLSTM_PIECES_PER_ATTN):
                next(mlstm, None)
    next(mlstm, None)
    y_b = _dot(att_scr[...], wb_ref[...]).astype(BF16)
    for _ in mlstm:
        pass

    y_c = _dot(yc_scr[...], wc_ref[...]).astype(BF16)
    merged = pa_ref[...] + sg_scr[:, :D_MODEL] * y_b + sg_scr[:, D_MODEL:] * y_c
    xo_ref[...] = x_ref[...] + _dot(merged, wo_ref[...])


def _mlp_kernel(x_ref, g_ref, gf_ref, wup_ref, wdn_ref, o_ref, *, final_norm):
    x = x_ref[...]
    h = _rmsnorm(x, g_ref[...]).astype(BF16)
    acc = x
    for c in range(D_FF // FF_CHUNK):
        cols = slice(c * FF_CHUNK, (c + 1) * FF_CHUNK)
        up = jnp.maximum(_dot(h, wup_ref[:, cols]), 0.0)
        acc = acc + _dot((up * up).astype(BF16), wdn_ref[cols, :])
    if final_norm:
        acc = _rmsnorm(acc, gf_ref[...])
    o_ref[...] = acc


def _resident(shape):
    nd = len(shape)
    return pl.BlockSpec(shape, lambda b, i: (0,) * nd, pipeline_mode=pl.Buffered(1))


def _layer_resident(shape, layer):
    nd = len(shape)
    return pl.BlockSpec((None,) + shape, lambda b, i: (layer,) + (0,) * nd,
                        pipeline_mode=pl.Buffered(1))


def _params():
    return pltpu.CompilerParams(dimension_semantics=("arbitrary", "arbitrary"),
                                vmem_limit_bytes=VMEM_LIMIT_BYTES)


def _mlstm_state_scratch():
    return [pltpu.VMEM((MLSTM_HEADS, MLSTM_HEAD_DIM, 2 * MLSTM_HEAD_DIM), F32),
            pltpu.VMEM((SUBLANES, LANES), F32)]


def _mix_in(x, pw, layer):
    bsz, seq, _ = x.shape
    tile = TILE
    n_tiles = seq // tile
    halo_per_tile = tile // SUBLANES
    n_halo = seq // SUBLANES

    def tok(width):
        return pl.BlockSpec((None, tile, width), lambda b, i: (b, i, 0))

    in_specs = [
        tok(D_MODEL),
        pl.BlockSpec((None, SUBLANES, D_MODEL),
                     lambda b, i: (b, jnp.maximum(i * halo_per_tile - 1, 0), 0)),
        pl.BlockSpec((None, SUBLANES, D_MODEL),
                     lambda b, i: (b, jnp.minimum((i + 1) * halo_per_tile, n_halo - 1), 0)),
        _layer_resident((1, D_MODEL), layer),
        _layer_resident((D_MODEL, N_PROJ), layer),
        _layer_resident((3, CONV_DIM), layer),
        _layer_resident((1, LANES), layer),
        _layer_resident((CONV_DIM, D_MODEL), layer),
    ]
    widths = (D_MODEL, ATTN_DIM, 2 * LANES, 2 * LANES, 3 * MLSTM_DIM, MLSTM_DIM)
    out_shape = [jax.ShapeDtypeStruct((bsz, seq, w), BF16) for w in widths]
    out_shape += [jax.ShapeDtypeStruct((bsz, 2 * SUBLANES, seq), F32),
                  jax.ShapeDtypeStruct((bsz, seq, MLSTM_DIM), F32)]
    out_specs = [tok(w) for w in widths]
    out_specs += [pl.BlockSpec((None, 2 * SUBLANES, tile), lambda b, i: (b, 0, i)), tok(MLSTM_DIM)]
    return pl.pallas_call(
        functools.partial(_mix_in_kernel, tile=tile),
        grid=(bsz, n_tiles),
        in_specs=in_specs,
        out_specs=out_specs,
        out_shape=out_shape,
        scratch_shapes=_mlstm_state_scratch(),
        compiler_params=_params(),
        name="mix_in",
    )(x, x, x, pw["g_mix"], pw["w_proj"], pw["conv_w"], pw["gate_b"], pw["w_a"])


def _mix_out(x, mid, pw, layer, attn_bias):
    pa, aq, kd, vd, mqkv, smo, grow, hf = mid
    bsz, seq, _ = x.shape
    tile = TILE
    n_tiles = seq // tile
    blk_per_tile = tile // ATTN_BLOCK
    n_blk = seq // ATTN_BLOCK

    def tok(width):
        return pl.BlockSpec((None, tile, width), lambda b, i: (b, n_tiles - 1 - i, 0))

    def blk_prev(width):
        return pl.BlockSpec(
            (None, ATTN_BLOCK, width),
            lambda b, i: (b, jnp.maximum((n_tiles - 1 - i) * blk_per_tile - 1, 0), 0))

    def blk_next(width):
        return pl.BlockSpec(
            (None, ATTN_BLOCK, width),
            lambda b, i: (b, jnp.minimum((n_tiles - i) * blk_per_tile, n_blk - 1), 0))

    in_specs = [
        pl.BlockSpec(memory_space=pltpu.SMEM),
        tok(D_MODEL), _layer_resident((1, D_MODEL), layer), tok(D_MODEL), tok(ATTN_DIM),
        tok(2 * LANES), blk_prev(2 * LANES), blk_next(2 * LANES),
        tok(2 * LANES), blk_prev(2 * LANES), blk_next(2 * LANES),
        tok(3 * MLSTM_DIM), tok(MLSTM_DIM),
        pl.BlockSpec((None, 2 * SUBLANES, tile), lambda b, i: (b, 0, n_tiles - 1 - i)),
        tok(MLSTM_DIM),
        _resident((ATTN_KV_HEADS, ATTN_GROUP * ATTN_BLOCK, 3 * ATTN_BLOCK)),
        _layer_resident((1, MLSTM_DIM), layer),
        _layer_resident((D_MODEL, 2 * D_MODEL), layer),
        _layer_resident((ATTN_DIM, D_MODEL), layer),
        _layer_resident((MLSTM_DIM, D_MODEL), layer),
        _layer_resident((D_MODEL, D_MODEL), layer),
    ]
    scratch = [pltpu.VMEM((tile + 2 * ATTN_BLOCK, 2 * LANES), BF16),
               pltpu.VMEM((tile + 2 * ATTN_BLOCK, 2 * LANES), BF16),
               pltpu.VMEM((tile, ATTN_DIM), BF16),
               pltpu.VMEM((tile, MLSTM_DIM), BF16),
               pltpu.VMEM((tile, 2 * D_MODEL), BF16)]
    return pl.pallas_call(
        functools.partial(_mix_out_kernel, tile=tile),
        grid=(bsz, n_tiles),
        in_specs=in_specs,
        out_specs=tok(D_MODEL),
        out_shape=jax.ShapeDtypeStruct((bsz, seq, D_MODEL), F32),
        scratch_shapes=scratch + _mlstm_state_scratch(),
        compiler_params=_params(),
        name="mix_out",
    )(pw["sink"][layer], x, pw["g_mix"], pa, aq, kd, kd, kd, vd, vd, vd, mqkv, smo, grow, hf,
      attn_bias, pw["mnorm_g"], pw["w_gbc"], pw["w_b"], pw["w_c"], pw["w_o"])


def _mlp(x, pw, layer, final_norm):
    bsz, seq, _ = x.shape
    tile = MLP_TILE
    tok = pl.BlockSpec((None, tile, D_MODEL), lambda b, i: (b, i, 0))
    return pl.pallas_call(
        functools.partial(_mlp_kernel, final_norm=final_norm),
        grid=(bsz, seq // tile),
        in_specs=[tok, _layer_resident((1, D_MODEL), layer), _resident((1, D_MODEL)),
                  _layer_resident((D_MODEL, D_FF), layer), _layer_resident((D_FF, D_MODEL), layer)],
        out_specs=tok,
        out_shape=jax.ShapeDtypeStruct((bsz, seq, D_MODEL), F32),
        compiler_params=_params(),
        name="mlp",
    )(x, pw["g_mlp"], pw["g_final"], pw["w_up"], pw["w_down"])


def _attention_bias():
    q = jnp.arange(ATTN_BLOCK)[:, None]
    k = jnp.arange(3 * ATTN_BLOCK)[None, :]
    dist = jnp.abs(k - ATTN_BLOCK - q).astype(F32)
    slopes = jnp.exp2(-8.0 * (jnp.arange(ATTN_HEADS, dtype=F32) + 1.0) / ATTN_HEADS)
    bias = jnp.where(dist <= WINDOW, -slopes[:, None, None] * dist, -jnp.inf)
    return bias.reshape(ATTN_KV_HEADS, ATTN_GROUP * ATTN_BLOCK, 3 * ATTN_BLOCK)


def _prepare_params(w_in, conv_w, attn_sink, mlstm_gate_b, mlstm_norm_g, w_out_a, w_out_b, w_out_c,
                    w_o, norm_mix_g, norm_mlp_g, w_mlp_up, w_mlp_down, norm_final_g):
    depth = w_in.shape[0]
    n_main = C_GA
    w_proj = jnp.concatenate(
        [w_in[:, :, :n_main].astype(BF16),
         w_in[:, :, n_main + N_GATES:n_main + N_GATES + D_MODEL].astype(BF16),
         w_in[:, :, n_main:n_main + N_GATES].astype(BF16),
         jnp.zeros((depth, D_MODEL, LANES - N_GATES), BF16)], axis=2)
    gate_b = jnp.concatenate([mlstm_gate_b, jnp.zeros((depth, LANES - N_GATES), F32)], axis=1)
    return {
        "w_proj": w_proj,
        "w_gbc": w_in[:, :, n_main + N_GATES + D_MODEL:].astype(BF16),
        "conv_w": conv_w,
        "sink": attn_sink,
        "gate_b": gate_b[:, None, :],
        "mnorm_g": mlstm_norm_g[:, None, :],
        "w_a": w_out_a.astype(BF16),
        "w_b": w_out_b.astype(BF16),
        "w_c": w_out_c.astype(BF16),
        "w_o": w_o.astype(BF16),
        "g_mix": norm_mix_g[:, None, :],
        "g_mlp": norm_mlp_g[:, None, :],
        "w_up": w_mlp_up.astype(BF16),
        "w_down": w_mlp_down.astype(BF16),
        "g_final": norm_final_g[None, :],
    }


def kernel(x_prompt, x_sample, w_in, conv_w, attn_sink, mlstm_gate_b, mlstm_norm_g, w_out_a,
           w_out_b, w_out_c, w_o, norm_mix_g, norm_mlp_g, w_mlp_up, w_mlp_down, norm_final_g):
    depth = w_in.shape[0]
    pw = _prepare_params(w_in, conv_w, attn_sink, mlstm_gate_b, mlstm_norm_g, w_out_a, w_out_b,
                         w_out_c, w_o, norm_mix_g, norm_mlp_g, w_mlp_up, w_mlp_down, norm_final_g)
    attn_bias = _attention_bias()

    def trunk(x):
        for layer in range(depth):
            x = _mix_out(x, _mix_in(x, pw, layer), pw, layer, attn_bias)
            x = _mlp(x, pw, layer, final_norm=(layer == depth - 1))
        return x

    return (trunk(x_prompt), trunk(x_sample))
```

```python
import functools

import jax
import jax.numpy as jnp
from jax import lax
from jax.experimental import pallas as pl
from jax.experimental.pallas import tpu as pltpu

F32 = jnp.float32
BF16 = jnp.bfloat16

LANES = 128
SUBLANES = 8
VMEM_LIMIT_BYTES = 56 * 1024 * 1024

D_MODEL = 1024
CONV_DIM = 512
ATTN_HEADS = 8
ATTN_KV_HEADS = 2
ATTN_GROUP = ATTN_HEADS // ATTN_KV_HEADS
ATTN_HEAD_DIM = 64
ATTN_DIM = ATTN_HEADS * ATTN_HEAD_DIM
ATTN_KV_DIM = ATTN_KV_HEADS * ATTN_HEAD_DIM
WINDOW = 128
ATTN_BLOCK = 128
MLSTM_HEADS = 4
MLSTM_HEAD_DIM = 128
MLSTM_DIM = MLSTM_HEADS * MLSTM_HEAD_DIM
MLSTM_CHUNK = 128
M_INIT = -1e30
D_FF = 4 * D_MODEL
RMS_EPS = 1e-6
LOG2E = 1.4426950408889634
N_GATES = 4 * MLSTM_HEADS

C_CONV = 0
C_AQ = C_CONV + 3 * CONV_DIM
C_AK = C_AQ + ATTN_DIM
C_AV = C_AK + ATTN_KV_DIM
C_ML = C_AV + ATTN_KV_DIM
C_GA = C_ML + 4 * MLSTM_DIM
C_GATE = C_GA + D_MODEL
N_PROJ = C_GATE + LANES

TILE = 512
MLP_TILE = 1024
FF_CHUNK = 1024
GA_BLOCK = 256
MLSTM_PIECES_PER_PROJ = 6
GATE_PIECES_IN_ATTN = 8
MLSTM_PIECES_PER_ATTN = 5


def _dot(a, b):
    return jnp.dot(a, b, preferred_element_type=F32)


def _dot_nt(a, b):
    return lax.dot_general(a, b, (((1,), (1,)), ((), ())), preferred_element_type=F32)


def _rmsnorm(x, g):
    r = lax.rsqrt(jnp.mean(x * x, axis=-1, keepdims=True) + RMS_EPS)
    return (x * r) * g


def _log_sigmoid(x):
    return jnp.minimum(x, 0.0) - jnp.log(1.0 + jnp.exp(-jnp.abs(x)))


def _mlstm_tile(qkv_ref, gate_rows, st_ref, m_ref, reverse, emit):
    L = MLSTM_CHUNK
    dh = MLSTM_HEAD_DIM
    n_chunks = len(gate_rows)
    order = list(reversed(range(n_chunks))) if reverse else list(range(n_chunks))
    row = lax.broadcasted_iota(jnp.int32, (L, L), 0)
    col = lax.broadcasted_iota(jnp.int32, (L, L), 1)
    keep = (col >= row) if reverse else (col <= row)
    tri = jnp.where((row >= col) if reverse else (row <= col), 1.0, 0.0).astype(BF16)
    ones = jnp.ones((L, dh), BF16)

    gates_all = jnp.concatenate(gate_rows, axis=0)
    logf_all = _log_sigmoid(jnp.concatenate(
        [pltpu.roll(gates, MLSTM_HEADS, axis=0) for gates in gate_rows], axis=0))
    hi = logf_all.astype(BF16)
    rest = logf_all - hi.astype(F32)
    mid = rest.astype(BF16)
    lo = (rest - mid.astype(F32)).astype(BF16)
    b_terms = _dot(jnp.concatenate([hi, mid, lo], axis=0), tri)
    n_rows = n_chunks * SUBLANES
    b_all = b_terms[:n_rows] + b_terms[n_rows:2 * n_rows] + b_terms[2 * n_rows:]
    r_all = gates_all - b_all
    g_all = b_all[:, 0:1] if reverse else b_all[:, L - 1:L]
    w_all = g_all + r_all
    m_loc_all = jnp.max(w_all, axis=1, keepdims=True)
    e_all = jnp.exp(w_all - m_loc_all)

    stats = {}
    m_prev = m_ref[...]
    for c in order:
        rows8 = slice(c * SUBLANES, (c + 1) * SUBLANES)
        g, m_loc = g_all[rows8], m_loc_all[rows8]
        m_new = jnp.maximum(g + m_prev, m_loc)
        stats[c] = (e_all[rows8], m_prev, jnp.exp(g + m_prev - m_new), jnp.exp(m_loc - m_new))
        m_prev = m_new
    m_ref[...] = m_prev
    yield

    logf2_all = logf_all * LOG2E
    r2_all = r_all * LOG2E
    work = {}
    for c in order:
        e_state, m_in, _, _ = stats[c]
        rows8 = slice(c * SUBLANES, (c + 1) * SUBLANES)
        logf2, r2, m_in2 = logf2_all[rows8], r2_all[rows8], m_in * LOG2E
        rows = slice(c * L, (c + 1) * L)
        for h in range(MLSTM_HEADS):
            qh = qkv_ref[rows, h * dh:(h + 1) * dh]
            kh = qkv_ref[rows, MLSTM_DIM + h * dh:MLSTM_DIM + (h + 1) * dh]
            vh = qkv_ref[rows, 2 * MLSTM_DIM + h * dh:2 * MLSTM_DIM + (h + 1) * dh]
            rm = jnp.where(keep, r2[h:h + 1, :], -jnp.inf)
            b_t = jnp.sum(jnp.where(keep, logf2[h:h + 1, :], 0.0), axis=1, keepdims=True)
            mp = m_in2[h:h + 1, 0:1]
            c_t = jnp.maximum(jnp.max(rm, axis=1, keepdims=True), mp)
            p = jnp.exp2(rm - c_t) * _dot_nt(qh, kh)
            a_t = jnp.exp2(mp - c_t)
            lhs = jnp.concatenate([p.astype(BF16), (a_t * qh.astype(F32)).astype(BF16)], axis=1)
            v1 = jnp.concatenate([vh, ones], axis=1)
            ke_t = (jnp.transpose(kh.astype(F32)) * e_state[h:h + 1, :]).astype(BF16)
            work[c, h] = (lhs, v1, jnp.exp2(-(b_t + c_t)), _dot(ke_t, v1))
            yield

    state = [st_ref[h] for h in range(MLSTM_HEADS)]
    for c in order:
        _, _, a_vec, c_vec = stats[c]
        for h in range(MLSTM_HEADS):
            lhs, v1, bound, upd = work[c, h]
            tot = _dot(lhs, jnp.concatenate([v1, state[h].astype(BF16)], axis=0))
            emit(c, h, tot[:, :dh] / jnp.maximum(jnp.abs(tot[:, dh:]), bound))
            state[h] = a_vec[h:h + 1, 0:1] * state[h] + c_vec[h:h + 1, 0:1] * upd
            yield
    for h in range(MLSTM_HEADS):
        st_ref[h] = state[h]


def _init_state(st_ref, m_ref):
    st_ref[...] = jnp.zeros(st_ref.shape, F32)
    m_ref[...] = jnp.full(m_ref.shape, M_INIT, F32)


def _mix_in_kernel(x_ref, xp_ref, xn_ref, gmix_ref, w_ref, convw_ref, gateb_ref, wa_ref,
                   hn_ref, pa_ref, aq_ref, kd_ref, vd_ref, mqkv_ref, smo_ref, grow_ref, hf_ref,
                   st_scr, m_scr, *, tile):
    i = pl.program_id(1)
    n_tiles = pl.num_programs(1)

    @pl.when(i == 0)
    def _():
        _init_state(st_scr, m_scr)

    g_mix = gmix_ref[...]
    h = _rmsnorm(x_ref[...], g_mix).astype(BF16)
    hn_ref[...] = h

    ul = _dot(h, w_ref[:, C_ML:C_ML + 4 * MLSTM_DIM])
    gcol = _dot(h, w_ref[:, C_GATE:C_GATE + LANES]) + gateb_ref[...]
    uc = _dot(h, w_ref[:, C_CONV:C_CONV + 3 * CONV_DIM])

    mqkv_ref[:, :MLSTM_DIM] = ul[:, :MLSTM_DIM].astype(BF16)
    mqkv_ref[:, MLSTM_DIM:2 * MLSTM_DIM] = (
        ul[:, MLSTM_DIM:2 * MLSTM_DIM] * (MLSTM_HEAD_DIM ** -0.5)).astype(BF16)
    mqkv_ref[:, 2 * MLSTM_DIM:] = ul[:, 2 * MLSTM_DIM:3 * MLSTM_DIM].astype(BF16)
    smo_ref[...] = jax.nn.sigmoid(ul[:, 3 * MLSTM_DIM:]).astype(BF16)
    gate_rows = []
    for c in range(tile // MLSTM_CHUNK):
        rows = slice(c * MLSTM_CHUNK, (c + 1) * MLSTM_CHUNK)
        g_rows = jnp.transpose(gcol[rows, :])[:2 * SUBLANES, :]
        grow_ref[:, rows] = g_rows
        gate_rows.append(g_rows[:SUBLANES, :])

    def emit_fwd(c, hd, out):
        hf_ref[c * MLSTM_CHUNK:(c + 1) * MLSTM_CHUNK, hd * MLSTM_HEAD_DIM:(hd + 1) * MLSTM_HEAD_DIM] = out

    mlstm = _mlstm_tile(mqkv_ref, gate_rows, st_scr, m_scr, False, emit_fwd)

    def conv_out():
        halo = jnp.concatenate([_rmsnorm(xp_ref[...], g_mix), _rmsnorm(xn_ref[...], g_mix)], axis=0)
        uh = _dot(halo.astype(BF16), w_ref[:, C_CONV + CONV_DIM:C_CONV + 3 * CONV_DIM])
        zh = uh[:, :CONV_DIM] * uh[:, CONV_DIM:]
        z_prev = jnp.where(i > 0, zh[SUBLANES - 1:SUBLANES, :], 0.0)
        z_next = jnp.where(i < n_tiles - 1, zh[SUBLANES:SUBLANES + 1, :], 0.0)
        z = uc[:, CONV_DIM:2 * CONV_DIM] * uc[:, 2 * CONV_DIM:]
        trow = lax.broadcasted_iota(jnp.int32, (tile, CONV_DIM), 0)
        z_dn = jnp.where(trow == 0, z_prev, pltpu.roll(z, 1, axis=0))
        z_up = jnp.where(trow == tile - 1, z_next, pltpu.roll(z, tile - 1, axis=0))
        conv = convw_ref[0:1, :] * z_dn + convw_ref[1:2, :] * z + convw_ref[2:3, :] * z_up
        return _dot((uc[:, :CONV_DIM] * conv).astype(BF16), wa_ref[...])

    def ga_block(y_a, blk):
        cols = slice(blk * GA_BLOCK, (blk + 1) * GA_BLOCK)
        sg = jax.nn.sigmoid(_dot(h, w_ref[:, C_GA + blk * GA_BLOCK:C_GA + (blk + 1) * GA_BLOCK]))
        pa_ref[:, cols] = (sg * y_a[:, cols]).astype(BF16)

    def attn_q():
        aq_ref[...] = (_dot(h, w_ref[:, C_AQ:C_AK]) * (ATTN_HEAD_DIM ** -0.5 * LOG2E)).astype(BF16)

    def attn_kv():
        ukv = _dot(h, w_ref[:, C_AK:C_AV + ATTN_KV_DIM])
        lane = lax.broadcasted_iota(jnp.int32, (tile, LANES), 1)
        low = lane < ATTN_HEAD_DIM
        for half, dst in ((ukv[:, :ATTN_KV_DIM], kd_ref), (ukv[:, ATTN_KV_DIM:], vd_ref)):
            swapped = pltpu.roll(half, ATTN_HEAD_DIM, axis=1)
            dst[:, :LANES] = jnp.where(low, half, swapped).astype(BF16)
            dst[:, LANES:] = jnp.where(low, swapped, half).astype(BF16)

    def advance():
        for _ in range(MLSTM_PIECES_PER_PROJ):
            next(mlstm, None)

    advance()
    y_a = conv_out()
    advance()
    n_ga = D_MODEL // GA_BLOCK
    for blk in range(n_ga):
        ga_block(y_a, blk)
        advance()
    attn_kv()
    for _ in mlstm:
        pass
    attn_q()


def _mix_out_kernel(sink_ref, x_ref, hn_ref, pa_ref, aq_ref, kd_ref, kdp_ref, kdn_ref,
                    vd_ref, vdp_ref, vdn_ref, mqkv_ref, smo_ref, grow_ref, hf_ref, bias_ref,
                    mnorm_ref, wg_ref, wb_ref, wc_ref, wo_ref,
                    xo_ref,
                    kwin, vwin, att_scr, yc_scr, sg_scr, st_scr, m_scr, *, tile):
    i = pl.program_id(1)
    n_tiles = pl.num_programs(1)
    ti = n_tiles - 1 - i
    n_qblk = tile // ATTN_BLOCK

    @pl.when(i == 0)
    def _():
        _init_state(st_scr, m_scr)

    kwin[0:ATTN_BLOCK, :] = kdp_ref[...]
    kwin[ATTN_BLOCK:ATTN_BLOCK + tile, :] = kd_ref[...]
    kwin[ATTN_BLOCK + tile:, :] = kdn_ref[...]
    vwin[0:ATTN_BLOCK, :] = vdp_ref[...]
    vwin[ATTN_BLOCK:ATTN_BLOCK + tile, :] = vd_ref[...]
    vwin[ATTN_BLOCK + tile:, :] = vdn_ref[...]

    h = hn_ref[...]
    lane = lax.broadcasted_iota(jnp.int32, (ATTN_BLOCK, LANES), 1)
    low = lane < ATTN_HEAD_DIM
    kcol = lax.broadcasted_iota(jnp.int32, (1, 3 * ATTN_BLOCK), 1)

    gate_cols = 2 * D_MODEL // (n_qblk * ATTN_KV_HEADS)

    def gate_block(blk):
        cols = slice(blk * gate_cols, (blk + 1) * gate_cols)
        sg_scr[:, cols] = jax.nn.sigmoid(_dot(h, wg_ref[:, cols])).astype(BF16)

    def attn_unit(j, g):
        rows = slice(j * ATTN_BLOCK, (j + 1) * ATTN_BLOCK)
        wrows = slice(j * ATTN_BLOCK, (j + 3) * ATTN_BLOCK)
        outside = None
        if j == 0:
            outside = jnp.logical_and(kcol < ATTN_BLOCK, ti == 0)
        if j == n_qblk - 1:
            after = jnp.logical_and(kcol >= 2 * ATTN_BLOCK, ti == n_tiles - 1)
            outside = after if outside is None else jnp.logical_or(outside, after)
        kw = kwin[wrows, g * LANES:(g + 1) * LANES]
        vw = vwin[wrows, g * LANES:(g + 1) * LANES]
        qs = []
        for pr in range(2):
            pair = g * 2 + pr
            qp = aq_ref[rows, pair * LANES:(pair + 1) * LANES]
            qs.append(jnp.where(low, qp, jnp.zeros_like(qp)))
            qs.append(jnp.where(low, jnp.zeros_like(qp), qp))
        bias = bias_ref[g]
        if outside is not None:
            bias = bias + jnp.where(outside, -jnp.inf, 0.0)
        s_all = _dot_nt(jnp.concatenate(qs, axis=0), kw) + bias
        if j * ATTN_KV_HEADS + g < GATE_PIECES_IN_ATTN:
            gate_block(j * ATTN_KV_HEADS + g)
        outs = []
        for idx in range(ATTN_GROUP):
            sk = sink_ref[g * ATTN_GROUP + idx] * LOG2E
            s = s_all[idx * ATTN_BLOCK:(idx + 1) * ATTN_BLOCK, :]
            m = jnp.maximum(jnp.max(s, axis=1, keepdims=True), sk)
            p = jnp.exp2(s - m)
            denom = jnp.sum(p, axis=1, keepdims=True) + jnp.exp2(sk - m)
            outs.append(_dot(p.astype(BF16), vw) / denom)
        for pr in range(2):
            pair = g * 2 + pr
            att_scr[rows, pair * LANES:(pair + 1) * LANES] = jnp.where(
                low, outs[2 * pr], outs[2 * pr + 1]).astype(BF16)

    def emit_bwd(c, hd, out):
        rows = slice(c * MLSTM_CHUNK, (c + 1) * MLSTM_CHUNK)
        sl = slice(hd * MLSTM_HEAD_DIM, (hd + 1) * MLSTM_HEAD_DIM)
        hm = hf_ref[rows, sl] + out
        hm = hm * lax.rsqrt(jnp.mean(hm * hm, axis=-1, keepdims=True) + RMS_EPS)
        yc_scr[rows, sl] = (smo_ref[rows, sl].astype(F32) * (hm * mnorm_ref[:, sl])).astype(BF16)

    gate_rows = [grow_ref[SUBLANES:, c * MLSTM_CHUNK:(c + 1) * MLSTM_CHUNK]
                 for c in range(tile // MLSTM_CHUNK)]
    mlstm = _mlstm_tile(mqkv_ref, gate_rows, st_scr, m_scr, True, emit_bwd)

    for j in range(n_qblk):
        for g in range(ATTN_KV_HEADS):
            attn_unit(j, g)
            for _ in range(MLSTM_PIECES_PER_ATTN):
                next(mlstm, None)
    next(mlstm, None)
    for blk in range(GATE_PIECES_IN_ATTN, n_qblk * ATTN_KV_HEADS):
        gate_block(blk)
    y_b = _dot(att_scr[...], wb_ref[...]).astype(BF16)
    for _ in mlstm:
        pass

    y_c = _dot(yc_scr[...], wc_ref[...]).astype(BF16)
    merged = pa_ref[...] + sg_scr[:, :D_MODEL] * y_b + sg_scr[:, D_MODEL:] * y_c
    xo_ref[...] = x_ref[...] + _dot(merged, wo_ref[...])


def _mlp_kernel(x_ref, g_ref, gf_ref, wup_ref, wdn_ref, o_ref, *, final_norm):
    x = x_ref[...]
    h = _rmsnorm(x, g_ref[...]).astype(BF16)
    acc = x
    for c in range(D_FF // FF_CHUNK):
        cols = slice(c * FF_CHUNK, (c + 1) * FF_CHUNK)
        up = jnp.maximum(_dot(h, wup_ref[:, cols]), 0.0)
        acc = acc + _dot((up * up).astype(BF16), wdn_ref[cols, :])
    if final_norm:
        acc = _rmsnorm(acc, gf_ref[...])
    o_ref[...] = acc


def _resident(shape):
    nd = len(shape)
    return pl.BlockSpec(shape, lambda b, i: (0,) * nd, pipeline_mode=pl.Buffered(1))


def _layer_resident(shape, layer):
    nd = len(shape)
    return pl.BlockSpec((None,) + shape, lambda b, i: (layer,) + (0,) * nd,
                        pipeline_mode=pl.Buffered(1))


def _params():
    return pltpu.CompilerParams(dimension_semantics=("arbitrary", "arbitrary"),
                                vmem_limit_bytes=VMEM_LIMIT_BYTES)


def _mlstm_state_scratch():
    return [pltpu.VMEM((MLSTM_HEADS, MLSTM_HEAD_DIM, 2 * MLSTM_HEAD_DIM), F32),
            pltpu.VMEM((SUBLANES, LANES), F32)]


def _mix_in(x, pw, layer):
    bsz, seq, _ = x.shape
    tile = TILE
    n_tiles = seq // tile
    halo_per_tile = tile // SUBLANES
    n_halo = seq // SUBLANES

    def tok(width):
        return pl.BlockSpec((None, tile, width), lambda b, i: (b, i, 0))

    in_specs = [
        tok(D_MODEL),
        pl.BlockSpec((None, SUBLANES, D_MODEL),
                     lambda b, i: (b, jnp.maximum(i * halo_per_tile - 1, 0), 0)),
        pl.BlockSpec((None, SUBLANES, D_MODEL),
                     lambda b, i: (b, jnp.minimum((i + 1) * halo_per_tile, n_halo - 1), 0)),
        _layer_resident((1, D_MODEL), layer),
        _layer_resident((D_MODEL, N_PROJ), layer),
        _layer_resident((3, CONV_DIM), layer),
        _layer_resident((1, LANES), layer),
        _layer_resident((CONV_DIM, D_MODEL), layer),
    ]
    widths = (D_MODEL, D_MODEL, ATTN_DIM, 2 * LANES, 2 * LANES, 3 * MLSTM_DIM, MLSTM_DIM)
    out_shape = [jax.ShapeDtypeStruct((bsz, seq, w), BF16) for w in widths]
    out_shape += [jax.ShapeDtypeStruct((bsz, 2 * SUBLANES, seq), F32),
                  jax.ShapeDtypeStruct((bsz, seq, MLSTM_DIM), F32)]
    out_specs = [tok(w) for w in widths]
    out_specs += [pl.BlockSpec((None, 2 * SUBLANES, tile), lambda b, i: (b, 0, i)), tok(MLSTM_DIM)]
    return pl.pallas_call(
        functools.partial(_mix_in_kernel, tile=tile),
        grid=(bsz, n_tiles),
        in_specs=in_specs,
        out_specs=out_specs,
        out_shape=out_shape,
        scratch_shapes=_mlstm_state_scratch(),
        compiler_params=_params(),
        name="mix_in",
    )(x, x, x, pw["g_mix"], pw["w_proj"], pw["conv_w"], pw["gate_b"], pw["w_a"])


def _mix_out(x, mid, pw, layer, attn_bias):
    hn, pa, aq, kd, vd, mqkv, smo, grow, hf = mid
    bsz, seq, _ = x.shape
    tile = TILE
    n_tiles = seq // tile
    blk_per_tile = tile // ATTN_BLOCK
    n_blk = seq // ATTN_BLOCK

    def tok(width):
        return pl.BlockSpec((None, tile, width), lambda b, i: (b, n_tiles - 1 - i, 0))

    def blk_prev(width):
        return pl.BlockSpec(
            (None, ATTN_BLOCK, width),
            lambda b, i: (b, jnp.maximum((n_tiles - 1 - i) * blk_per_tile - 1, 0), 0))

    def blk_next(width):
        return pl.BlockSpec(
            (None, ATTN_BLOCK, width),
            lambda b, i: (b, jnp.minimum((n_tiles - i) * blk_per_tile, n_blk - 1), 0))

    in_specs = [
        pl.BlockSpec(memory_space=pltpu.SMEM),
        tok(D_MODEL), tok(D_MODEL), tok(D_MODEL), tok(ATTN_DIM),
        tok(2 * LANES), blk_prev(2 * LANES), blk_next(2 * LANES),
        tok(2 * LANES), blk_prev(2 * LANES), blk_next(2 * LANES),
        tok(3 * MLSTM_DIM), tok(MLSTM_DIM),
        pl.BlockSpec((None, 2 * SUBLANES, tile), lambda b, i: (b, 0, n_tiles - 1 - i)),
        tok(MLSTM_DIM),
        _resident((ATTN_KV_HEADS, ATTN_GROUP * ATTN_BLOCK, 3 * ATTN_BLOCK)),
        _layer_resident((1, MLSTM_DIM), layer),
        _layer_resident((D_MODEL, 2 * D_MODEL), layer),
        _layer_resident((ATTN_DIM, D_MODEL), layer),
        _layer_resident((MLSTM_DIM, D_MODEL), layer),
        _layer_resident((D_MODEL, D_MODEL), layer),
    ]
    scratch = [pltpu.VMEM((tile + 2 * ATTN_BLOCK, 2 * LANES), BF16),
               pltpu.VMEM((tile + 2 * ATTN_BLOCK, 2 * LANES), BF16),
               pltpu.VMEM((tile, ATTN_DIM), BF16),
               pltpu.VMEM((tile, MLSTM_DIM), BF16),
               pltpu.VMEM((tile, 2 * D_MODEL), BF16)]
    return pl.pallas_call(
        functools.partial(_mix_out_kernel, tile=tile),
        grid=(bsz, n_tiles),
        in_specs=in_specs,
        out_specs=tok(D_MODEL),
        out_shape=jax.ShapeDtypeStruct((bsz, seq, D_MODEL), F32),
        scratch_shapes=scratch + _mlstm_state_scratch(),
        compiler_params=_params(),
        name="mix_out",
    )(pw["sink"][layer], x, hn, pa, aq, kd, kd, kd, vd, vd, vd, mqkv, smo, grow, hf,
      attn_bias, pw["mnorm_g"], pw["w_gbc"], pw["w_b"], pw["w_c"], pw["w_o"])


def _mlp(x, pw, layer, final_norm):
    bsz, seq, _ = x.shape
    tile = MLP_TILE
    tok = pl.BlockSpec((None, tile, D_MODEL), lambda b, i: (b, i, 0))
    return pl.pallas_call(
        functools.partial(_mlp_kernel, final_norm=final_norm),
        grid=(bsz, seq // tile),
        in_specs=[tok, _layer_resident((1, D_MODEL), layer), _resident((1, D_MODEL)),
                  _layer_resident((D_MODEL, D_FF), layer), _layer_resident((D_FF, D_MODEL), layer)],
        out_specs=tok,
        out_shape=jax.ShapeDtypeStruct((bsz, seq, D_MODEL), F32),
        compiler_params=_params(),
        name="mlp",
    )(x, pw["g_mlp"], pw["g_final"], pw["w_up"], pw["w_down"])


def _attention_bias():
    q = jnp.arange(ATTN_BLOCK)[:, None]
    k = jnp.arange(3 * ATTN_BLOCK)[None, :]
    dist = jnp.abs(k - ATTN_BLOCK - q).astype(F32)
    slopes = jnp.exp2(-8.0 * (jnp.arange(ATTN_HEADS, dtype=F32) + 1.0) / ATTN_HEADS)
    bias = jnp.where(dist <= WINDOW, -slopes[:, None, None] * dist * LOG2E, -jnp.inf)
    return bias.reshape(ATTN_KV_HEADS, ATTN_GROUP * ATTN_BLOCK, 3 * ATTN_BLOCK)


def _prepare_params(w_in, conv_w, attn_sink, mlstm_gate_b, mlstm_norm_g, w_out_a, w_out_b, w_out_c,
                    w_o, norm_mix_g, norm_mlp_g, w_mlp_up, w_mlp_down, norm_final_g):
    depth = w_in.shape[0]
    n_main = C_GA
    w16 = w_in.astype(BF16)
    w_proj = jnp.concatenate(
        [w16[:, :, :n_main], w16[:, :, n_main + N_GATES:n_main + N_GATES + D_MODEL],
         w16[:, :, n_main:n_main + N_GATES], jnp.zeros((depth, D_MODEL, LANES - N_GATES), BF16)],
        axis=2)
    gate_b = jnp.concatenate([mlstm_gate_b, jnp.zeros((depth, LANES - N_GATES), F32)], axis=1)
    return {
        "w_proj": w_proj,
        "w_gbc": w16[:, :, n_main + N_GATES + D_MODEL:],
        "conv_w": conv_w,
        "sink": attn_sink,
        "gate_b": gate_b[:, None, :],
        "mnorm_g": mlstm_norm_g[:, None, :],
        "w_a": w_out_a.astype(BF16),
        "w_b": w_out_b.astype(BF16),
        "w_c": w_out_c.astype(BF16),
        "w_o": w_o.astype(BF16),
        "g_mix": norm_mix_g[:, None, :],
        "g_mlp": norm_mlp_g[:, None, :],
        "w_up": w_mlp_up.astype(BF16),
        "w_down": w_mlp_down.astype(BF16),
        "g_final": norm_final_g[None, :],
    }


def kernel(x_prompt, x_sample, w_in, conv_w, attn_sink, mlstm_gate_b, mlstm_norm_g, w_out_a,
           w_out_b, w_out_c, w_o, norm_mix_g, norm_mlp_g, w_mlp_up, w_mlp_down, norm_final_g):
    depth = w_in.shape[0]
    pw = _prepare_params(w_in, conv_w, attn_sink, mlstm_gate_b, mlstm_norm_g, w_out_a, w_out_b,
                         w_out_c, w_o, norm_mix_g, norm_mlp_g, w_mlp_up, w_mlp_down, norm_final_g)
    attn_bias = _attention_bias()

    def trunk(x):
        for layer in range(depth):
            x = _mix_out(x, _mix_in(x, pw, layer), pw, layer, attn_bias)
            x = _mlp(x, pw, layer, final_norm=(layer == depth - 1))
        return x

    return (trunk(x_prompt), trunk(x_sample))
```

```python
import functools

import jax
import jax.numpy as jnp
from jax import lax
from jax.experimental import pallas as pl
from jax.experimental.pallas import tpu as pltpu

F32 = jnp.float32
BF16 = jnp.bfloat16

LANES = 128
SUBLANES = 8
VMEM_LIMIT_BYTES = 56 * 1024 * 1024

D_MODEL = 1024
CONV_DIM = 512
ATTN_HEADS = 8
ATTN_KV_HEADS = 2
ATTN_GROUP = ATTN_HEADS // ATTN_KV_HEADS
ATTN_HEAD_DIM = 64
ATTN_DIM = ATTN_HEADS * ATTN_HEAD_DIM
ATTN_KV_DIM = ATTN_KV_HEADS * ATTN_HEAD_DIM
WINDOW = 128
ATTN_BLOCK = 128
MLSTM_HEADS = 4
MLSTM_HEAD_DIM = 128
MLSTM_DIM = MLSTM_HEADS * MLSTM_HEAD_DIM
MLSTM_CHUNK = 128
M_INIT = -1e30
D_FF = 4 * D_MODEL
RMS_EPS = 1e-6
LOG2E = 1.4426950408889634
N_GATES = 4 * MLSTM_HEADS

C_CONV = 0
C_AQ = C_CONV + 3 * CONV_DIM
C_AK = C_AQ + ATTN_DIM
C_AV = C_AK + ATTN_KV_DIM
C_ML = C_AV + ATTN_KV_DIM
C_GA = C_ML + 4 * MLSTM_DIM
C_GATE = C_GA + D_MODEL
N_PROJ = C_GATE + LANES

TILE = 512
MLP_TILE = 1024
FF_CHUNK = 1024
SPLIT_ROWS = 256
GA_BLOCK = 256
MLSTM_PIECES_PER_PROJ = 6
GATE_PIECES_IN_ATTN = 8
MLSTM_PIECES_PER_ATTN = 5


def _dot(a, b):
    return jnp.dot(a, b, preferred_element_type=F32)


def _dot_nt(a, b):
    return lax.dot_general(a, b, (((1,), (1,)), ((), ())), preferred_element_type=F32)


def _rmsnorm(x, g):
    r = lax.rsqrt(jnp.mean(x * x, axis=-1, keepdims=True) + RMS_EPS)
    return (x * r) * g


def _log_sigmoid(x):
    return jnp.minimum(x, 0.0) - jnp.log(1.0 + jnp.exp(-jnp.abs(x)))


def _mlstm_tile(qkv_ref, gate_rows, st_ref, m_ref, reverse, emit):
    L = MLSTM_CHUNK
    dh = MLSTM_HEAD_DIM
    n_chunks = len(gate_rows)
    order = list(reversed(range(n_chunks))) if reverse else list(range(n_chunks))
    row = lax.broadcasted_iota(jnp.int32, (L, L), 0)
    col = lax.broadcasted_iota(jnp.int32, (L, L), 1)
    keep = (col >= row) if reverse else (col <= row)
    tri = jnp.where((row >= col) if reverse else (row <= col), 1.0, 0.0).astype(BF16)
    ones = jnp.ones((L, dh), BF16)

    gates_all = jnp.concatenate(gate_rows, axis=0)
    logf_all = _log_sigmoid(jnp.concatenate(
        [pltpu.roll(gates, MLSTM_HEADS, axis=0) for gates in gate_rows], axis=0))
    hi = logf_all.astype(BF16)
    rest = logf_all - hi.astype(F32)
    mid = rest.astype(BF16)
    lo = (rest - mid.astype(F32)).astype(BF16)
    b_terms = _dot(jnp.concatenate([hi, mid, lo], axis=0), tri)
    n_rows = n_chunks * SUBLANES
    b_all = b_terms[:n_rows] + b_terms[n_rows:2 * n_rows] + b_terms[2 * n_rows:]
    r_all = gates_all - b_all
    g_all = b_all[:, 0:1] if reverse else b_all[:, L - 1:L]
    w_all = g_all + r_all
    m_loc_all = jnp.max(w_all, axis=1, keepdims=True)
    e_all = jnp.exp(w_all - m_loc_all)

    stats = {}
    m_prev = m_ref[...]
    for c in order:
        rows8 = slice(c * SUBLANES, (c + 1) * SUBLANES)
        g, m_loc = g_all[rows8], m_loc_all[rows8]
        m_new = jnp.maximum(g + m_prev, m_loc)
        stats[c] = (e_all[rows8], m_prev, jnp.exp(g + m_prev - m_new), jnp.exp(m_loc - m_new))
        m_prev = m_new
    m_ref[...] = m_prev
    yield

    logf2_all = logf_all * LOG2E
    r2_all = r_all * LOG2E
    work = {}
    for c in order:
        e_state, m_in, _, _ = stats[c]
        rows8 = slice(c * SUBLANES, (c + 1) * SUBLANES)
        logf2, r2, m_in2 = logf2_all[rows8], r2_all[rows8], m_in * LOG2E
        rows = slice(c * L, (c + 1) * L)
        for h in range(MLSTM_HEADS):
            qh = qkv_ref[rows, h * dh:(h + 1) * dh]
            kh = qkv_ref[rows, MLSTM_DIM + h * dh:MLSTM_DIM + (h + 1) * dh]
            vh = qkv_ref[rows, 2 * MLSTM_DIM + h * dh:2 * MLSTM_DIM + (h + 1) * dh]
            rm = jnp.where(keep, r2[h:h + 1, :], -jnp.inf)
            b_t = jnp.sum(jnp.where(keep, logf2[h:h + 1, :], 0.0), axis=1, keepdims=True)
            mp = m_in2[h:h + 1, 0:1]
            c_t = jnp.maximum(jnp.max(rm, axis=1, keepdims=True), mp)
            p = jnp.exp2(rm - c_t) * _dot_nt(qh, kh)
            a_t = jnp.exp2(mp - c_t)
            lhs = jnp.concatenate([p.astype(BF16), (a_t * qh.astype(F32)).astype(BF16)], axis=1)
            v1 = jnp.concatenate([vh, ones], axis=1)
            ke_t = (jnp.transpose(kh.astype(F32)) * e_state[h:h + 1, :]).astype(BF16)
            work[c, h] = (lhs, v1, jnp.exp2(-(b_t + c_t)), _dot(ke_t, v1))
            yield

    state = [st_ref[h] for h in range(MLSTM_HEADS)]
    for c in order:
        _, _, a_vec, c_vec = stats[c]
        for h in range(MLSTM_HEADS):
            lhs, v1, bound, upd = work[c, h]
            tot = _dot(lhs, jnp.concatenate([v1, state[h].astype(BF16)], axis=0))
            emit(c, h, tot[:, :dh] / jnp.maximum(jnp.abs(tot[:, dh:]), bound))
            state[h] = a_vec[h:h + 1, 0:1] * state[h] + c_vec[h:h + 1, 0:1] * upd
            yield
    for h in range(MLSTM_HEADS):
        st_ref[h] = state[h]


def _init_state(st_ref, m_ref):
    st_ref[...] = jnp.zeros(st_ref.shape, F32)
    m_ref[...] = jnp.full(m_ref.shape, M_INIT, F32)


def _mix_in_kernel(x_ref, xp_ref, xn_ref, gmix_ref, w_ref, convw_ref, gateb_ref, wa_ref,
                   hn_ref, pa_ref, aq_ref, kd_ref, vd_ref, mqkv_ref, smo_ref, grow_ref, hf_ref,
                   st_scr, m_scr, *, tile):
    i = pl.program_id(1)
    n_tiles = pl.num_programs(1)

    @pl.when(i == 0)
    def _():
        _init_state(st_scr, m_scr)

    g_mix = gmix_ref[...]
    h = _rmsnorm(x_ref[...], g_mix).astype(BF16)
    hn_ref[...] = h

    ul = _dot(h, w_ref[:, C_ML:C_ML + 4 * MLSTM_DIM])
    gcol = _dot(h, w_ref[:, C_GATE:C_GATE + LANES]) + gateb_ref[...]
    uc = _dot(h, w_ref[:, C_CONV:C_CONV + 3 * CONV_DIM])

    mqkv_ref[:, :MLSTM_DIM] = ul[:, :MLSTM_DIM].astype(BF16)
    mqkv_ref[:, MLSTM_DIM:2 * MLSTM_DIM] = (
        ul[:, MLSTM_DIM:2 * MLSTM_DIM] * (MLSTM_HEAD_DIM ** -0.5)).astype(BF16)
    mqkv_ref[:, 2 * MLSTM_DIM:] = ul[:, 2 * MLSTM_DIM:3 * MLSTM_DIM].astype(BF16)
    smo_ref[...] = jax.nn.sigmoid(ul[:, 3 * MLSTM_DIM:]).astype(BF16)
    gate_rows = []
    for c in range(tile // MLSTM_CHUNK):
        rows = slice(c * MLSTM_CHUNK, (c + 1) * MLSTM_CHUNK)
        g_rows = jnp.transpose(gcol[rows, :])[:2 * SUBLANES, :]
        grow_ref[:, rows] = g_rows
        gate_rows.append(g_rows[:SUBLANES, :])

    def emit_fwd(c, hd, out):
        hf_ref[c * MLSTM_CHUNK:(c + 1) * MLSTM_CHUNK, hd * MLSTM_HEAD_DIM:(hd + 1) * MLSTM_HEAD_DIM] = out

    mlstm = _mlstm_tile(mqkv_ref, gate_rows, st_scr, m_scr, False, emit_fwd)

    def conv_out():
        halo = jnp.concatenate([_rmsnorm(xp_ref[...], g_mix), _rmsnorm(xn_ref[...], g_mix)], axis=0)
        uh = _dot(halo.astype(BF16), w_ref[:, C_CONV + CONV_DIM:C_CONV + 3 * CONV_DIM])
        zh = uh[:, :CONV_DIM] * uh[:, CONV_DIM:]
        z_prev = jnp.where(i > 0, zh[SUBLANES - 1:SUBLANES, :], 0.0)
        z_next = jnp.where(i < n_tiles - 1, zh[SUBLANES:SUBLANES + 1, :], 0.0)
        z = uc[:, CONV_DIM:2 * CONV_DIM] * uc[:, 2 * CONV_DIM:]
        trow = lax.broadcasted_iota(jnp.int32, (tile, CONV_DIM), 0)
        z_dn = jnp.where(trow == 0, z_prev, pltpu.roll(z, 1, axis=0))
        z_up = jnp.where(trow == tile - 1, z_next, pltpu.roll(z, tile - 1, axis=0))
        conv = convw_ref[0:1, :] * z_dn + convw_ref[1:2, :] * z + convw_ref[2:3, :] * z_up
        return _dot((uc[:, :CONV_DIM] * conv).astype(BF16), wa_ref[...])

    def ga_block(y_a, blk):
        cols = slice(blk * GA_BLOCK, (blk + 1) * GA_BLOCK)
        sg = jax.nn.sigmoid(_dot(h, w_ref[:, C_GA + blk * GA_BLOCK:C_GA + (blk + 1) * GA_BLOCK]))
        pa_ref[:, cols] = (sg * y_a[:, cols]).astype(BF16)

    def attn_q():
        aq_ref[...] = (_dot(h, w_ref[:, C_AQ:C_AK]) * (ATTN_HEAD_DIM ** -0.5 * LOG2E)).astype(BF16)

    def attn_kv():
        ukv = _dot(h, w_ref[:, C_AK:C_AV + ATTN_KV_DIM])
        lane = lax.broadcasted_iota(jnp.int32, (tile, LANES), 1)
        low = lane < ATTN_HEAD_DIM
        for half, dst in ((ukv[:, :ATTN_KV_DIM], kd_ref), (ukv[:, ATTN_KV_DIM:], vd_ref)):
            swapped = pltpu.roll(half, ATTN_HEAD_DIM, axis=1)
            dst[:, :LANES] = jnp.where(low, half, swapped).astype(BF16)
            dst[:, LANES:] = jnp.where(low, swapped, half).astype(BF16)

    def advance():
        for _ in range(MLSTM_PIECES_PER_PROJ):
            next(mlstm, None)

    advance()
    y_a = conv_out()
    advance()
    n_ga = D_MODEL // GA_BLOCK
    for blk in range(n_ga):
        ga_block(y_a, blk)
        advance()
    attn_kv()
    for _ in mlstm:
        pass
    attn_q()


def _mix_out_kernel(sink_ref, x_ref, hn_ref, pa_ref, aq_ref, kd_ref, kdp_ref, kdn_ref,
                    vd_ref, vdp_ref, vdn_ref, mqkv_ref, smo_ref, grow_ref, hf_ref, bias_ref,
                    mnorm_ref, wg_ref, wb_ref, wc_ref, wo_ref,
                    xo_ref,
                    kwin, vwin, att_scr, yc_scr, sg_scr, st_scr, m_scr, *, tile):
    i = pl.program_id(1)
    n_tiles = pl.num_programs(1)
    ti = n_tiles - 1 - i
    n_qblk = tile // ATTN_BLOCK

    @pl.when(i == 0)
    def _():
        _init_state(st_scr, m_scr)

    kwin[0:ATTN_BLOCK, :] = kdp_ref[...]
    kwin[ATTN_BLOCK:ATTN_BLOCK + tile, :] = kd_ref[...]
    kwin[ATTN_BLOCK + tile:, :] = kdn_ref[...]
    vwin[0:ATTN_BLOCK, :] = vdp_ref[...]
    vwin[ATTN_BLOCK:ATTN_BLOCK + tile, :] = vd_ref[...]
    vwin[ATTN_BLOCK + tile:, :] = vdn_ref[...]

    h = hn_ref[...]
    lane = lax.broadcasted_iota(jnp.int32, (ATTN_BLOCK, LANES), 1)
    low = lane < ATTN_HEAD_DIM
    kcol = lax.broadcasted_iota(jnp.int32, (1, 3 * ATTN_BLOCK), 1)

    gate_cols = 2 * D_MODEL // (n_qblk * ATTN_KV_HEADS)

    def gate_block(blk):
        cols = slice(blk * gate_cols, (blk + 1) * gate_cols)
        sg_scr[:, cols] = jax.nn.sigmoid(_dot(h, wg_ref[:, cols])).astype(BF16)

    def attn_unit(j, g):
        rows = slice(j * ATTN_BLOCK, (j + 1) * ATTN_BLOCK)
        wrows = slice(j * ATTN_BLOCK, (j + 3) * ATTN_BLOCK)
        outside = None
        if j == 0:
            outside = jnp.logical_and(kcol < ATTN_BLOCK, ti == 0)
        if j == n_qblk - 1:
            after = jnp.logical_and(kcol >= 2 * ATTN_BLOCK, ti == n_tiles - 1)
            outside = after if outside is None else jnp.logical_or(outside, after)
        kw = kwin[wrows, g * LANES:(g + 1) * LANES]
        vw = vwin[wrows, g * LANES:(g + 1) * LANES]
        qs = []
        for pr in range(2):
            pair = g * 2 + pr
            qp = aq_ref[rows, pair * LANES:(pair + 1) * LANES]
            qs.append(jnp.where(low, qp, jnp.zeros_like(qp)))
            qs.append(jnp.where(low, jnp.zeros_like(qp), qp))
        bias = bias_ref[g]
        if outside is not None:
            bias = bias + jnp.where(outside, -jnp.inf, 0.0)
        s_all = _dot_nt(jnp.concatenate(qs, axis=0), kw) + bias
        if j * ATTN_KV_HEADS + g < GATE_PIECES_IN_ATTN:
            gate_block(j * ATTN_KV_HEADS + g)
        outs = []
        for idx in range(ATTN_GROUP):
            sk = sink_ref[g * ATTN_GROUP + idx] * LOG2E
            s = s_all[idx * ATTN_BLOCK:(idx + 1) * ATTN_BLOCK, :]
            m = jnp.maximum(jnp.max(s, axis=1, keepdims=True), sk)
            p = jnp.exp2(s - m)
            denom = jnp.sum(p, axis=1, keepdims=True) + jnp.exp2(sk - m)
            outs.append(_dot(p.astype(BF16), vw) / denom)
        for pr in range(2):
            pair = g * 2 + pr
            att_scr[rows, pair * LANES:(pair + 1) * LANES] = jnp.where(
                low, outs[2 * pr], outs[2 * pr + 1]).astype(BF16)

    def emit_bwd(c, hd, out):
        rows = slice(c * MLSTM_CHUNK, (c + 1) * MLSTM_CHUNK)
        sl = slice(hd * MLSTM_HEAD_DIM, (hd + 1) * MLSTM_HEAD_DIM)
        hm = hf_ref[rows, sl] + out
        hm = hm * lax.rsqrt(jnp.mean(hm * hm, axis=-1, keepdims=True) + RMS_EPS)
        yc_scr[rows, sl] = (smo_ref[rows, sl].astype(F32) * (hm * mnorm_ref[:, sl])).astype(BF16)

    gate_rows = [grow_ref[SUBLANES:, c * MLSTM_CHUNK:(c + 1) * MLSTM_CHUNK]
                 for c in range(tile // MLSTM_CHUNK)]
    mlstm = _mlstm_tile(mqkv_ref, gate_rows, st_scr, m_scr, True, emit_bwd)

    for j in range(n_qblk):
        for g in range(ATTN_KV_HEADS):
            attn_unit(j, g)
            for _ in range(MLSTM_PIECES_PER_ATTN):
                next(mlstm, None)
    next(mlstm, None)
    for blk in range(GATE_PIECES_IN_ATTN, n_qblk * ATTN_KV_HEADS):
        gate_block(blk)
    y_b = _dot(att_scr[...], wb_ref[...]).astype(BF16)
    for _ in mlstm:
        pass

    y_c = _dot(yc_scr[...], wc_ref[...]).astype(BF16)
    merged = pa_ref[...] + sg_scr[:, :D_MODEL] * y_b + sg_scr[:, D_MODEL:] * y_c
    xo_ref[...] = x_ref[...] + _dot(merged, wo_ref[...])


def _mlp_kernel(x_ref, g_ref, gf_ref, wup_ref, wdn_ref, o_ref, *, final_norm):
    x = x_ref[...]
    h = _rmsnorm(x, g_ref[...]).astype(BF16)
    acc = x
    for c in range(D_FF // FF_CHUNK):
        cols = slice(c * FF_CHUNK, (c + 1) * FF_CHUNK)
        up = jnp.maximum(_dot(h, wup_ref[:, cols]), 0.0)
        acc = acc + _dot((up * up).astype(BF16), wdn_ref[cols, :])
    if final_norm:
        acc = _rmsnorm(acc, gf_ref[...])
    o_ref[...] = acc


def _resident(shape):
    nd = len(shape)
    return pl.BlockSpec(shape, lambda b, i: (0,) * nd, pipeline_mode=pl.Buffered(1))


def _layer_resident(shape, layer):
    nd = len(shape)
    return pl.BlockSpec((None,) + shape, lambda b, i: (layer,) + (0,) * nd,
                        pipeline_mode=pl.Buffered(1))


def _params():
    return pltpu.CompilerParams(dimension_semantics=("arbitrary", "arbitrary"),
                                vmem_limit_bytes=VMEM_LIMIT_BYTES)


def _mlstm_state_scratch():
    return [pltpu.VMEM((MLSTM_HEADS, MLSTM_HEAD_DIM, 2 * MLSTM_HEAD_DIM), F32),
            pltpu.VMEM((SUBLANES, LANES), F32)]


def _mix_in(x, pw, layer):
    bsz, seq, _ = x.shape
    tile = TILE
    n_tiles = seq // tile
    halo_per_tile = tile // SUBLANES
    n_halo = seq // SUBLANES

    def tok(width):
        return pl.BlockSpec((None, tile, width), lambda b, i: (b, i, 0))

    in_specs = [
        tok(D_MODEL),
        pl.BlockSpec((None, SUBLANES, D_MODEL),
                     lambda b, i: (b, jnp.maximum(i * halo_per_tile - 1, 0), 0)),
        pl.BlockSpec((None, SUBLANES, D_MODEL),
                     lambda b, i: (b, jnp.minimum((i + 1) * halo_per_tile, n_halo - 1), 0)),
        _layer_resident((1, D_MODEL), layer),
        _layer_resident((D_MODEL, N_PROJ), layer),
        _layer_resident((3, CONV_DIM), layer),
        _layer_resident((1, LANES), layer),
        _layer_resident((CONV_DIM, D_MODEL), layer),
    ]
    widths = (D_MODEL, D_MODEL, ATTN_DIM, 2 * LANES, 2 * LANES, 3 * MLSTM_DIM, MLSTM_DIM)
    out_shape = [jax.ShapeDtypeStruct((bsz, seq, w), BF16) for w in widths]
    out_shape += [jax.ShapeDtypeStruct((bsz, 2 * SUBLANES, seq), F32),
                  jax.ShapeDtypeStruct((bsz, seq, MLSTM_DIM), F32)]
    out_specs = [tok(w) for w in widths]
    out_specs += [pl.BlockSpec((None, 2 * SUBLANES, tile), lambda b, i: (b, 0, i)), tok(MLSTM_DIM)]
    return pl.pallas_call(
        functools.partial(_mix_in_kernel, tile=tile),
        grid=(bsz, n_tiles),
        in_specs=in_specs,
        out_specs=out_specs,
        out_shape=out_shape,
        scratch_shapes=_mlstm_state_scratch(),
        compiler_params=_params(),
        name="mix_in",
    )(x, x, x, pw["g_mix"], pw["w_proj"], pw["conv_w"], pw["gate_b"], pw["w_a"])


def _mix_out(x, mid, pw, layer, attn_bias):
    hn, pa, aq, kd, vd, mqkv, smo, grow, hf = mid
    bsz, seq, _ = x.shape
    tile = TILE
    n_tiles = seq // tile
    blk_per_tile = tile // ATTN_BLOCK
    n_blk = seq // ATTN_BLOCK

    def tok(width):
        return pl.BlockSpec((None, tile, width), lambda b, i: (b, n_tiles - 1 - i, 0))

    def blk_prev(width):
        return pl.BlockSpec(
            (None, ATTN_BLOCK, width),
            lambda b, i: (b, jnp.maximum((n_tiles - 1 - i) * blk_per_tile - 1, 0), 0))

    def blk_next(width):
        return pl.BlockSpec(
            (None, ATTN_BLOCK, width),
            lambda b, i: (b, jnp.minimum((n_tiles - i) * blk_per_tile, n_blk - 1), 0))

    in_specs = [
        pl.BlockSpec(memory_space=pltpu.SMEM),
        tok(D_MODEL), tok(D_MODEL), tok(D_MODEL), tok(ATTN_DIM),
        tok(2 * LANES), blk_prev(2 * LANES), blk_next(2 * LANES),
        tok(2 * LANES), blk_prev(2 * LANES), blk_next(2 * LANES),
        tok(3 * MLSTM_DIM), tok(MLSTM_DIM),
        pl.BlockSpec((None, 2 * SUBLANES, tile), lambda b, i: (b, 0, n_tiles - 1 - i)),
        tok(MLSTM_DIM),
        _resident((ATTN_KV_HEADS, ATTN_GROUP * ATTN_BLOCK, 3 * ATTN_BLOCK)),
        _layer_resident((1, MLSTM_DIM), layer),
        _layer_resident((D_MODEL, 2 * D_MODEL), layer),
        _layer_resident((ATTN_DIM, D_MODEL), layer),
        _layer_resident((MLSTM_DIM, D_MODEL), layer),
        _layer_resident((D_MODEL, D_MODEL), layer),
    ]
    scratch = [pltpu.VMEM((tile + 2 * ATTN_BLOCK, 2 * LANES), BF16),
               pltpu.VMEM((tile + 2 * ATTN_BLOCK, 2 * LANES), BF16),
               pltpu.VMEM((tile, ATTN_DIM), BF16),
               pltpu.VMEM((tile, MLSTM_DIM), BF16),
               pltpu.VMEM((tile, 2 * D_MODEL), BF16)]
    return pl.pallas_call(
        functools.partial(_mix_out_kernel, tile=tile),
        grid=(bsz, n_tiles),
        in_specs=in_specs,
        out_specs=tok(D_MODEL),
        out_shape=jax.ShapeDtypeStruct((bsz, seq, D_MODEL), F32),
        scratch_shapes=scratch + _mlstm_state_scratch(),
        compiler_params=_params(),
        name="mix_out",
    )(pw["sink"][layer], x, hn, pa, aq, kd, kd, kd, vd, vd, vd, mqkv, smo, grow, hf,
      attn_bias, pw["mnorm_g"], pw["w_gbc"], pw["w_b"], pw["w_c"], pw["w_o"])


def _mlp(x, pw, layer, final_norm):
    bsz, seq, _ = x.shape
    tile = MLP_TILE
    tok = pl.BlockSpec((None, tile, D_MODEL), lambda b, i: (b, i, 0))
    return pl.pallas_call(
        functools.partial(_mlp_kernel, final_norm=final_norm),
        grid=(bsz, seq // tile),
        in_specs=[tok, _layer_resident((1, D_MODEL), layer), _resident((1, D_MODEL)),
                  _layer_resident((D_MODEL, D_FF), layer), _layer_resident((D_FF, D_MODEL), layer)],
        out_specs=tok,
        out_shape=jax.ShapeDtypeStruct((bsz, seq, D_MODEL), F32),
        compiler_params=_params(),
        name="mlp",
    )(x, pw["g_mlp"], pw["g_final"], pw["w_up"], pw["w_down"])


def _attention_bias():
    q = jnp.arange(ATTN_BLOCK)[:, None]
    k = jnp.arange(3 * ATTN_BLOCK)[None, :]
    dist = jnp.abs(k - ATTN_BLOCK - q).astype(F32)
    slopes = jnp.exp2(-8.0 * (jnp.arange(ATTN_HEADS, dtype=F32) + 1.0) / ATTN_HEADS)
    bias = jnp.where(dist <= WINDOW, -slopes[:, None, None] * dist * LOG2E, -jnp.inf)
    return bias.reshape(ATTN_KV_HEADS, ATTN_GROUP * ATTN_BLOCK, 3 * ATTN_BLOCK)


def _split_kernel(w_ref, proj_ref, gbc_ref):
    n_main = C_GA
    proj_ref[:, :n_main] = w_ref[:, :n_main].astype(BF16)
    proj_ref[:, C_GA:C_GATE] = w_ref[:, n_main + N_GATES:n_main + N_GATES + D_MODEL].astype(BF16)
    gates = w_ref[:, n_main:n_main + N_GATES].astype(BF16)
    proj_ref[:, C_GATE:] = jnp.concatenate(
        [gates, jnp.zeros((gates.shape[0], LANES - N_GATES), BF16)], axis=1)
    gbc_ref[...] = w_ref[:, n_main + N_GATES + D_MODEL:].astype(BF16)


def _split_input_projection(w_in):
    depth, rows, n_in = w_in.shape
    blk = SPLIT_ROWS
    return pl.pallas_call(
        _split_kernel,
        grid=(depth, rows // blk),
        in_specs=[pl.BlockSpec((None, blk, n_in), lambda l, i: (l, i, 0))],
        out_specs=[pl.BlockSpec((None, blk, N_PROJ), lambda l, i: (l, i, 0)),
                   pl.BlockSpec((None, blk, 2 * D_MODEL), lambda l, i: (l, i, 0))],
        out_shape=[jax.ShapeDtypeStruct((depth, rows, N_PROJ), BF16),
                   jax.ShapeDtypeStruct((depth, rows, 2 * D_MODEL), BF16)],
        compiler_params=_params(),
        name="split_w_in",
    )(w_in)


def _prepare_params(w_in, conv_w, attn_sink, mlstm_gate_b, mlstm_norm_g, w_out_a, w_out_b, w_out_c,
                    w_o, norm_mix_g, norm_mlp_g, w_mlp_up, w_mlp_down, norm_final_g):
    depth = w_in.shape[0]
    w_proj, w_gbc = _split_input_projection(w_in)
    gate_b = jnp.concatenate([mlstm_gate_b, jnp.zeros((depth, LANES - N_GATES), F32)], axis=1)
    return {
        "w_proj": w_proj,
        "w_gbc": w_gbc,
        "conv_w": conv_w,
        "sink": attn_sink,
        "gate_b": gate_b[:, None, :],
        "mnorm_g": mlstm_norm_g[:, None, :],
        "w_a": w_out_a.astype(BF16),
        "w_b": w_out_b.astype(BF16),
        "w_c": w_out_c.astype(BF16),
        "w_o": w_o.astype(BF16),
        "g_mix": norm_mix_g[:, None, :],
        "g_mlp": norm_mlp_g[:, None, :],
        "w_up": w_mlp_up.astype(BF16),
        "w_down": w_mlp_down.astype(BF16),
        "g_final": norm_final_g[None, :],
    }


def kernel(x_prompt, x_sample, w_in, conv_w, attn_sink, mlstm_gate_b, mlstm_norm_g, w_out_a,
           w_out_b, w_out_c, w_o, norm_mix_g, norm_mlp_g, w_mlp_up, w_mlp_down, norm_final_g):
    depth = w_in.shape[0]
    pw = _prepare_params(w_in, conv_w, attn_sink, mlstm_gate_b, mlstm_norm_g, w_out_a, w_out_b,
                         w_out_c, w_o, norm_mix_g, norm_mlp_g, w_mlp_up, w_mlp_down, norm_final_g)
    attn_bias = _attention_bias()

    def trunk(x):
        for layer in range(depth):
            x = _mix_out(x, _mix_in(x, pw, layer), pw, layer, attn_bias)
            x = _mlp(x, pw, layer, final_norm=(layer == depth - 1))
        return x

    return (trunk(x_prompt), trunk(x_sample))
```

```python
import functools

import jax
import jax.numpy as jnp
from jax import lax
from jax.experimental import pallas as pl
from jax.experimental.pallas import tpu as pltpu

F32 = jnp.float32
BF16 = jnp.bfloat16

LANES = 128
SUBLANES = 8
VMEM_LIMIT_BYTES = 56 * 1024 * 1024

D_MODEL = 1024
CONV_DIM = 512
ATTN_HEADS = 8
ATTN_KV_HEADS = 2
ATTN_GROUP = ATTN_HEADS // ATTN_KV_HEADS
ATTN_HEAD_DIM = 64
ATTN_DIM = ATTN_HEADS * ATTN_HEAD_DIM
ATTN_KV_DIM = ATTN_KV_HEADS * ATTN_HEAD_DIM
WINDOW = 128
ATTN_BLOCK = 128
MLSTM_HEADS = 4
MLSTM_HEAD_DIM = 128
MLSTM_DIM = MLSTM_HEADS * MLSTM_HEAD_DIM
MLSTM_CHUNK = 128
M_INIT = -1e30
D_FF = 4 * D_MODEL
RMS_EPS = 1e-6
LOG2E = 1.4426950408889634
N_GATES = 4 * MLSTM_HEADS

C_CONV = 0
C_AQ = C_CONV + 3 * CONV_DIM
C_AK = C_AQ + ATTN_DIM
C_AV = C_AK + ATTN_KV_DIM
C_ML = C_AV + ATTN_KV_DIM
C_GA = C_ML + 4 * MLSTM_DIM
C_GATE = C_GA + D_MODEL
N_PROJ = C_GATE + LANES

TILE = 512
MLP_TILE = 1024
FF_CHUNK = 1024
SPLIT_ROWS = 256
GA_BLOCK = 256
MLSTM_PIECES_PER_PROJ = 6
GATE_PIECES_IN_ATTN = 8
MLSTM_PIECES_PER_ATTN = 4


def _dot(a, b):
    return jnp.dot(a, b, preferred_element_type=F32)


def _dot_nt(a, b):
    return lax.dot_general(a, b, (((1,), (1,)), ((), ())), preferred_element_type=F32)


def _rmsnorm(x, g):
    r = lax.rsqrt(jnp.mean(x * x, axis=-1, keepdims=True) + RMS_EPS)
    return (x * r) * g


def _log_sigmoid(x):
    return jnp.minimum(x, 0.0) - jnp.log(1.0 + jnp.exp(-jnp.abs(x)))


def _mlstm_tile(qkv_ref, gate_rows, st_ref, m_ref, reverse, emit):
    L = MLSTM_CHUNK
    dh = MLSTM_HEAD_DIM
    n_chunks = len(gate_rows)
    order = list(reversed(range(n_chunks))) if reverse else list(range(n_chunks))
    row = lax.broadcasted_iota(jnp.int32, (L, L), 0)
    col = lax.broadcasted_iota(jnp.int32, (L, L), 1)
    keep = (col >= row) if reverse else (col <= row)
    tri = jnp.where((row >= col) if reverse else (row <= col), 1.0, 0.0).astype(BF16)
    ones = jnp.ones((L, dh), BF16)

    gates_all = jnp.concatenate(gate_rows, axis=0)
    logf_all = _log_sigmoid(jnp.concatenate(
        [pltpu.roll(gates, MLSTM_HEADS, axis=0) for gates in gate_rows], axis=0))
    hi = logf_all.astype(BF16)
    rest = logf_all - hi.astype(F32)
    mid = rest.astype(BF16)
    lo = (rest - mid.astype(F32)).astype(BF16)
    b_terms = _dot(jnp.concatenate([hi, mid, lo], axis=0), tri)
    n_rows = n_chunks * SUBLANES
    b_all = b_terms[:n_rows] + b_terms[n_rows:2 * n_rows] + b_terms[2 * n_rows:]
    r_all = gates_all - b_all
    g_all = b_all[:, 0:1] if reverse else b_all[:, L - 1:L]
    w_all = g_all + r_all
    m_loc_all = jnp.max(w_all, axis=1, keepdims=True)
    e_all = jnp.exp(w_all - m_loc_all)

    stats = {}
    m_prev = m_ref[...]
    for c in order:
        rows8 = slice(c * SUBLANES, (c + 1) * SUBLANES)
        g, m_loc = g_all[rows8], m_loc_all[rows8]
        m_new = jnp.maximum(g + m_prev, m_loc)
        stats[c] = (e_all[rows8], m_prev, jnp.exp(g + m_prev - m_new), jnp.exp(m_loc - m_new))
        m_prev = m_new
    m_ref[...] = m_prev
    yield

    logf2_all = logf_all * LOG2E
    r2_all = r_all * LOG2E
    work = {}
    for c in order:
        e_state, m_in, _, _ = stats[c]
        rows8 = slice(c * SUBLANES, (c + 1) * SUBLANES)
        logf2, r2, m_in2 = logf2_all[rows8], r2_all[rows8], m_in * LOG2E
        rows = slice(c * L, (c + 1) * L)
        for h in range(MLSTM_HEADS):
            qh = qkv_ref[rows, h * dh:(h + 1) * dh]
            kh = qkv_ref[rows, MLSTM_DIM + h * dh:MLSTM_DIM + (h + 1) * dh]
            vh = qkv_ref[rows, 2 * MLSTM_DIM + h * dh:2 * MLSTM_DIM + (h + 1) * dh]
            rm = jnp.where(keep, r2[h:h + 1, :], -jnp.inf)
            b_t = jnp.sum(jnp.where(keep, logf2[h:h + 1, :], 0.0), axis=1, keepdims=True)
            mp = m_in2[h:h + 1, 0:1]
            c_t = jnp.maximum(jnp.max(rm, axis=1, keepdims=True), mp)
            p = jnp.exp2(rm - c_t) * _dot_nt(qh, kh)
            a_t = jnp.exp2(mp - c_t)
            lhs = jnp.concatenate([p.astype(BF16), (a_t * qh.astype(F32)).astype(BF16)], axis=1)
            v1 = jnp.concatenate([vh, ones], axis=1)
            ke_t = (jnp.transpose(kh.astype(F32)) * e_state[h:h + 1, :]).astype(BF16)
            work[c, h] = (lhs, v1, jnp.exp2(-(b_t + c_t)), _dot(ke_t, v1))
            yield

    state = [st_ref[h] for h in range(MLSTM_HEADS)]
    for c in order:
        _, _, a_vec, c_vec = stats[c]
        for h in range(MLSTM_HEADS):
            lhs, v1, bound, upd = work[c, h]
            tot = _dot(lhs, jnp.concatenate([v1, state[h].astype(BF16)], axis=0))
            emit(c, h, tot[:, :dh] / jnp.maximum(jnp.abs(tot[:, dh:]), bound))
            state[h] = a_vec[h:h + 1, 0:1] * state[h] + c_vec[h:h + 1, 0:1] * upd
            yield
    for h in range(MLSTM_HEADS):
        st_ref[h] = state[h]


def _init_state(st_ref, m_ref):
    st_ref[...] = jnp.zeros(st_ref.shape, F32)
    m_ref[...] = jnp.full(m_ref.shape, M_INIT, F32)


def _mix_in_kernel(x_ref, xp_ref, xn_ref, gmix_ref, w_ref, wvt_ref, convw_ref, gateb_ref, wa_ref,
                   hn_ref, pa_ref, aq_ref, kd_ref, vt_ref, mqkv_ref, smo_ref, grow_ref, hf_ref,
                   st_scr, m_scr, *, tile):
    i = pl.program_id(1)
    n_tiles = pl.num_programs(1)

    @pl.when(i == 0)
    def _():
        _init_state(st_scr, m_scr)

    g_mix = gmix_ref[...]
    h = _rmsnorm(x_ref[...], g_mix).astype(BF16)
    hn_ref[...] = h

    ul = _dot(h, w_ref[:, C_ML:C_ML + 4 * MLSTM_DIM])
    gcol = _dot(h, w_ref[:, C_GATE:C_GATE + LANES]) + gateb_ref[...]
    uc = _dot(h, w_ref[:, C_CONV:C_CONV + 3 * CONV_DIM])

    mqkv_ref[:, :MLSTM_DIM] = ul[:, :MLSTM_DIM].astype(BF16)
    mqkv_ref[:, MLSTM_DIM:2 * MLSTM_DIM] = (
        ul[:, MLSTM_DIM:2 * MLSTM_DIM] * (MLSTM_HEAD_DIM ** -0.5)).astype(BF16)
    mqkv_ref[:, 2 * MLSTM_DIM:] = ul[:, 2 * MLSTM_DIM:3 * MLSTM_DIM].astype(BF16)
    smo_ref[...] = jax.nn.sigmoid(ul[:, 3 * MLSTM_DIM:]).astype(BF16)
    gate_rows = []
    for c in range(tile // MLSTM_CHUNK):
        rows = slice(c * MLSTM_CHUNK, (c + 1) * MLSTM_CHUNK)
        g_rows = jnp.transpose(gcol[rows, :])[:2 * SUBLANES, :]
        grow_ref[:, rows] = g_rows
        gate_rows.append(g_rows[:SUBLANES, :])

    def emit_fwd(c, hd, out):
        hf_ref[c * MLSTM_CHUNK:(c + 1) * MLSTM_CHUNK, hd * MLSTM_HEAD_DIM:(hd + 1) * MLSTM_HEAD_DIM] = out

    mlstm = _mlstm_tile(mqkv_ref, gate_rows, st_scr, m_scr, False, emit_fwd)

    def conv_out():
        halo = jnp.concatenate([_rmsnorm(xp_ref[...], g_mix), _rmsnorm(xn_ref[...], g_mix)], axis=0)
        uh = _dot(halo.astype(BF16), w_ref[:, C_CONV + CONV_DIM:C_CONV + 3 * CONV_DIM])
        zh = uh[:, :CONV_DIM] * uh[:, CONV_DIM:]
        z_prev = jnp.where(i > 0, zh[SUBLANES - 1:SUBLANES, :], 0.0)
        z_next = jnp.where(i < n_tiles - 1, zh[SUBLANES:SUBLANES + 1, :], 0.0)
        z = uc[:, CONV_DIM:2 * CONV_DIM] * uc[:, 2 * CONV_DIM:]
        trow = lax.broadcasted_iota(jnp.int32, (tile, CONV_DIM), 0)
        z_dn = jnp.where(trow == 0, z_prev, pltpu.roll(z, 1, axis=0))
        z_up = jnp.where(trow == tile - 1, z_next, pltpu.roll(z, tile - 1, axis=0))
        conv = convw_ref[0:1, :] * z_dn + convw_ref[1:2, :] * z + convw_ref[2:3, :] * z_up
        return _dot((uc[:, :CONV_DIM] * conv).astype(BF16), wa_ref[...])

    def ga_block(y_a, blk):
        cols = slice(blk * GA_BLOCK, (blk + 1) * GA_BLOCK)
        sg = jax.nn.sigmoid(_dot(h, w_ref[:, C_GA + blk * GA_BLOCK:C_GA + (blk + 1) * GA_BLOCK]))
        pa_ref[:, cols] = (sg * y_a[:, cols]).astype(BF16)

    def attn_q():
        aq_ref[...] = (_dot(h, w_ref[:, C_AQ:C_AK]) * (ATTN_HEAD_DIM ** -0.5 * LOG2E)).astype(BF16)

    def attn_kv():
        uk = _dot(h, w_ref[:, C_AK:C_AK + ATTN_KV_DIM])
        lane = lax.broadcasted_iota(jnp.int32, (tile, LANES), 1)
        low = lane < ATTN_HEAD_DIM
        swapped = pltpu.roll(uk, ATTN_HEAD_DIM, axis=1)
        kd_ref[:, :LANES] = jnp.where(low, uk, swapped).astype(BF16)
        kd_ref[:, LANES:] = jnp.where(low, swapped, uk).astype(BF16)
        vt_ref[...] = _dot_nt(wvt_ref[...], h).astype(BF16)

    def advance():
        for _ in range(MLSTM_PIECES_PER_PROJ):
            next(mlstm, None)

    advance()
    y_a = conv_out()
    advance()
    n_ga = D_MODEL // GA_BLOCK
    for blk in range(n_ga):
        ga_block(y_a, blk)
        advance()
    attn_kv()
    for _ in mlstm:
        pass
    attn_q()


def _mix_out_kernel(sink_ref, x_ref, hn_ref, pa_ref, aq_ref, kd_ref, kdp_ref, kdn_ref,
                    vt_ref, vtp_ref, vtn_ref, mqkv_ref, smo_ref, grow_ref, hf_ref, bias_ref,
                    mnorm_ref, wg_ref, wb_ref, wc_ref, wo_ref,
                    xo_ref,
                    kwin, vtwin, att_scr, yc_scr, sg_scr, st_scr, m_scr, *, tile):
    i = pl.program_id(1)
    n_tiles = pl.num_programs(1)
    ti = n_tiles - 1 - i
    n_qblk = tile // ATTN_BLOCK

    @pl.when(i == 0)
    def _():
        _init_state(st_scr, m_scr)

    kwin[0:ATTN_BLOCK, :] = kdp_ref[...]
    kwin[ATTN_BLOCK:ATTN_BLOCK + tile, :] = kd_ref[...]
    kwin[ATTN_BLOCK + tile:, :] = kdn_ref[...]
    vtwin[:, 0:ATTN_BLOCK] = vtp_ref[...]
    vtwin[:, ATTN_BLOCK:ATTN_BLOCK + tile] = vt_ref[...]
    vtwin[:, ATTN_BLOCK + tile:] = vtn_ref[...]

    h = hn_ref[...]
    lane = lax.broadcasted_iota(jnp.int32, (ATTN_BLOCK, LANES), 1)
    low = lane < ATTN_HEAD_DIM
    krow = lax.broadcasted_iota(jnp.int32, (3 * ATTN_BLOCK, 2 * ATTN_BLOCK), 0)
    second = lax.broadcasted_iota(jnp.int32, (1, 2 * ATTN_BLOCK), 1) >= ATTN_BLOCK

    gate_cols = 2 * D_MODEL // (n_qblk * ATTN_KV_HEADS)

    def gate_block(blk):
        cols = slice(blk * gate_cols, (blk + 1) * gate_cols)
        sg_scr[:, cols] = jax.nn.sigmoid(_dot(h, wg_ref[:, cols])).astype(BF16)

    def attn_scores(j, g):
        rows = slice(j * ATTN_BLOCK, (j + 1) * ATTN_BLOCK)
        wrows = slice(j * ATTN_BLOCK, (j + 3) * ATTN_BLOCK)
        outside = None
        if j == 0:
            outside = jnp.logical_and(krow < ATTN_BLOCK, ti == 0)
        if j == n_qblk - 1:
            after = jnp.logical_and(krow >= 2 * ATTN_BLOCK, ti == n_tiles - 1)
            outside = after if outside is None else jnp.logical_or(outside, after)
        kw = kwin[wrows, g * LANES:(g + 1) * LANES]
        scores = []
        for pr in range(2):
            pair = g * 2 + pr
            qp = aq_ref[rows, pair * LANES:(pair + 1) * LANES]
            zero = jnp.zeros_like(qp)
            q_both = jnp.concatenate([jnp.where(low, qp, zero), jnp.where(low, zero, qp)], axis=0)
            s = _dot_nt(kw, q_both) + bias_ref[pair]
            if outside is not None:
                s = jnp.where(outside, -jnp.inf, s)
            scores.append(s)
        if j * ATTN_KV_HEADS + g < GATE_PIECES_IN_ATTN:
            gate_block(j * ATTN_KV_HEADS + g)
        return scores

    def attn_finish(j, g, scores):
        rows = slice(j * ATTN_BLOCK, (j + 1) * ATTN_BLOCK)
        wrows = slice(j * ATTN_BLOCK, (j + 3) * ATTN_BLOCK)
        vt = vtwin[g * ATTN_HEAD_DIM:(g + 1) * ATTN_HEAD_DIM, wrows]
        for pr, s in enumerate(scores):
            pair = g * 2 + pr
            sk = jnp.where(second, sink_ref[2 * pair + 1] * LOG2E, sink_ref[2 * pair] * LOG2E)
            m = jnp.maximum(jnp.max(s, axis=0, keepdims=True), sk)
            p = jnp.exp2(s - m)
            denom = jnp.sum(p, axis=0, keepdims=True) + jnp.exp2(sk - m)
            o_t = _dot(vt, p.astype(BF16)) / denom
            o_pair = jnp.concatenate([o_t[:, :ATTN_BLOCK], o_t[:, ATTN_BLOCK:]], axis=0)
            att_scr[rows, pair * LANES:(pair + 1) * LANES] = jnp.transpose(o_pair).astype(BF16)

    def emit_bwd(c, hd, out):
        rows = slice(c * MLSTM_CHUNK, (c + 1) * MLSTM_CHUNK)
        sl = slice(hd * MLSTM_HEAD_DIM, (hd + 1) * MLSTM_HEAD_DIM)
        hm = hf_ref[rows, sl] + out
        hm = hm * lax.rsqrt(jnp.mean(hm * hm, axis=-1, keepdims=True) + RMS_EPS)
        yc_scr[rows, sl] = (smo_ref[rows, sl].astype(F32) * (hm * mnorm_ref[:, sl])).astype(BF16)

    gate_rows = [grow_ref[SUBLANES:, c * MLSTM_CHUNK:(c + 1) * MLSTM_CHUNK]
                 for c in range(tile // MLSTM_CHUNK)]
    mlstm = _mlstm_tile(mqkv_ref, gate_rows, st_scr, m_scr, True, emit_bwd)

    units = [(j, g) for j in range(n_qblk) for g in range(ATTN_KV_HEADS)]
    scores = attn_scores(*units[0])
    for k, unit in enumerate(units):
        following = attn_scores(*units[k + 1]) if k + 1 < len(units) else None
        attn_finish(*unit, scores)
        for _ in range(MLSTM_PIECES_PER_ATTN):
            next(mlstm, None)
        scores = following
    next(mlstm, None)
    for blk in range(GATE_PIECES_IN_ATTN, n_qblk * ATTN_KV_HEADS):
        gate_block(blk)
    y_b = _dot(att_scr[...], wb_ref[...]).astype(BF16)
    for _ in mlstm:
        pass

    y_c = _dot(yc_scr[...], wc_ref[...]).astype(BF16)
    merged = pa_ref[...] + sg_scr[:, :D_MODEL] * y_b + sg_scr[:, D_MODEL:] * y_c
    xo_ref[...] = x_ref[...] + _dot(merged, wo_ref[...])


def _mlp_kernel(x_ref, g_ref, gf_ref, wup_ref, wdn_ref, o_ref, *, final_norm):
    x = x_ref[...]
    h = _rmsnorm(x, g_ref[...]).astype(BF16)
    acc = x
    for c in range(D_FF // FF_CHUNK):
        cols = slice(c * FF_CHUNK, (c + 1) * FF_CHUNK)
        up = jnp.maximum(_dot(h, wup_ref[:, cols]), 0.0)
        acc = acc + _dot((up * up).astype(BF16), wdn_ref[cols, :])
    if final_norm:
        acc = _rmsnorm(acc, gf_ref[...])
    o_ref[...] = acc


def _resident(shape):
    nd = len(shape)
    return pl.BlockSpec(shape, lambda b, i: (0,) * nd, pipeline_mode=pl.Buffered(1))


def _layer_resident(shape, layer):
    nd = len(shape)
    return pl.BlockSpec((None,) + shape, lambda b, i: (layer,) + (0,) * nd,
                        pipeline_mode=pl.Buffered(1))


def _params():
    return pltpu.CompilerParams(dimension_semantics=("arbitrary", "arbitrary"),
                                vmem_limit_bytes=VMEM_LIMIT_BYTES)


def _mlstm_state_scratch():
    return [pltpu.VMEM((MLSTM_HEADS, MLSTM_HEAD_DIM, 2 * MLSTM_HEAD_DIM), F32),
            pltpu.VMEM((SUBLANES, LANES), F32)]


def _mix_in(x, pw, layer):
    bsz, seq, _ = x.shape
    tile = TILE
    n_tiles = seq // tile
    halo_per_tile = tile // SUBLANES
    n_halo = seq // SUBLANES

    def tok(width):
        return pl.BlockSpec((None, tile, width), lambda b, i: (b, i, 0))

    in_specs = [
        tok(D_MODEL),
        pl.BlockSpec((None, SUBLANES, D_MODEL),
                     lambda b, i: (b, jnp.maximum(i * halo_per_tile - 1, 0), 0)),
        pl.BlockSpec((None, SUBLANES, D_MODEL),
                     lambda b, i: (b, jnp.minimum((i + 1) * halo_per_tile, n_halo - 1), 0)),
        _layer_resident((1, D_MODEL), layer),
        _layer_resident((D_MODEL, N_PROJ), layer),
        _layer_resident((ATTN_KV_DIM, D_MODEL), layer),
        _layer_resident((3, CONV_DIM), layer),
        _layer_resident((1, LANES), layer),
        _layer_resident((CONV_DIM, D_MODEL), layer),
    ]
    def tok_t(rows):
        return pl.BlockSpec((None, rows, tile), lambda b, i: (b, 0, i))

    def shape(width, dtype=BF16):
        return jax.ShapeDtypeStruct((bsz, seq, width), dtype)

    def shape_t(rows, dtype):
        return jax.ShapeDtypeStruct((bsz, rows, seq), dtype)

    out_shape = [shape(D_MODEL), shape(D_MODEL), shape(ATTN_DIM), shape(2 * LANES),
                 shape_t(ATTN_KV_DIM, BF16), shape(3 * MLSTM_DIM), shape(MLSTM_DIM),
                 shape_t(N_GATES, F32), shape(MLSTM_DIM, F32)]
    out_specs = [tok(D_MODEL), tok(D_MODEL), tok(ATTN_DIM), tok(2 * LANES),
                 tok_t(ATTN_KV_DIM), tok(3 * MLSTM_DIM), tok(MLSTM_DIM),
                 tok_t(N_GATES), tok(MLSTM_DIM)]
    return pl.pallas_call(
        functools.partial(_mix_in_kernel, tile=tile),
        grid=(bsz, n_tiles),
        in_specs=in_specs,
        out_specs=out_specs,
        out_shape=out_shape,
        scratch_shapes=_mlstm_state_scratch(),
        compiler_params=_params(),
        name="mix_in",
    )(x, x, x, pw["g_mix"], pw["w_proj"], pw["w_vt"], pw["conv_w"], pw["gate_b"], pw["w_a"])


def _mix_out(x, mid, pw, layer, attn_bias):
    hn, pa, aq, kd, vt, mqkv, smo, grow, hf = mid
    bsz, seq, _ = x.shape
    tile = TILE
    n_tiles = seq // tile
    blk_per_tile = tile // ATTN_BLOCK
    n_blk = seq // ATTN_BLOCK

    def tok(width):
        return pl.BlockSpec((None, tile, width), lambda b, i: (b, n_tiles - 1 - i, 0))

    def blk_prev(width):
        return pl.BlockSpec(
            (None, ATTN_BLOCK, width),
            lambda b, i: (b, jnp.maximum((n_tiles - 1 - i) * blk_per_tile - 1, 0), 0))

    def blk_next(width):
        return pl.BlockSpec(
            (None, ATTN_BLOCK, width),
            lambda b, i: (b, jnp.minimum((n_tiles - i) * blk_per_tile, n_blk - 1), 0))

    in_specs = [
        pl.BlockSpec(memory_space=pltpu.SMEM),
        tok(D_MODEL), tok(D_MODEL), tok(D_MODEL), tok(ATTN_DIM),
        tok(2 * LANES), blk_prev(2 * LANES), blk_next(2 * LANES),
        pl.BlockSpec((None, ATTN_KV_DIM, tile), lambda b, i: (b, 0, n_tiles - 1 - i)),
        pl.BlockSpec((None, ATTN_KV_DIM, ATTN_BLOCK),
                     lambda b, i: (b, 0, jnp.maximum((n_tiles - 1 - i) * blk_per_tile - 1, 0))),
        pl.BlockSpec((None, ATTN_KV_DIM, ATTN_BLOCK),
                     lambda b, i: (b, 0, jnp.minimum((n_tiles - i) * blk_per_tile, n_blk - 1))),
        tok(3 * MLSTM_DIM), tok(MLSTM_DIM),
        pl.BlockSpec((None, 2 * SUBLANES, tile), lambda b, i: (b, 0, n_tiles - 1 - i)),
        tok(MLSTM_DIM),
        _resident((ATTN_HEADS // 2, 3 * ATTN_BLOCK, 2 * ATTN_BLOCK)),
        _layer_resident((1, MLSTM_DIM), layer),
        _layer_resident((D_MODEL, 2 * D_MODEL), layer),
        _layer_resident((ATTN_DIM, D_MODEL), layer),
        _layer_resident((MLSTM_DIM, D_MODEL), layer),
        _layer_resident((D_MODEL, D_MODEL), layer),
    ]
    scratch = [pltpu.VMEM((tile + 2 * ATTN_BLOCK, 2 * LANES), BF16),
               pltpu.VMEM((ATTN_KV_DIM, tile + 2 * ATTN_BLOCK), BF16),
               pltpu.VMEM((tile, ATTN_DIM), BF16),
               pltpu.VMEM((tile, MLSTM_DIM), BF16),
               pltpu.VMEM((tile, 2 * D_MODEL), BF16)]
    return pl.pallas_call(
        functools.partial(_mix_out_kernel, tile=tile),
        grid=(bsz, n_tiles),
        in_specs=in_specs,
        out_specs=tok(D_MODEL),
        out_shape=jax.ShapeDtypeStruct((bsz, seq, D_MODEL), F32),
        scratch_shapes=scratch + _mlstm_state_scratch(),
        compiler_params=_params(),
        name="mix_out",
    )(pw["sink"][layer], x, hn, pa, aq, kd, kd, kd, vt, vt, vt, mqkv, smo, grow, hf,
      attn_bias, pw["mnorm_g"], pw["w_gbc"], pw["w_b"], pw["w_c"], pw["w_o"])


def _mlp(x, pw, layer, final_norm):
    bsz, seq, _ = x.shape
    tile = MLP_TILE
    tok = pl.BlockSpec((None, tile, D_MODEL), lambda b, i: (b, i, 0))
    return pl.pallas_call(
        functools.partial(_mlp_kernel, final_norm=final_norm),
        grid=(bsz, seq // tile),
        in_specs=[tok, _layer_resident((1, D_MODEL), layer), _resident((1, D_MODEL)),
                  _layer_resident((D_MODEL, D_FF), layer), _layer_resident((D_FF, D_MODEL), layer)],
        out_specs=tok,
        out_shape=jax.ShapeDtypeStruct((bsz, seq, D_MODEL), F32),
        compiler_params=_params(),
        name="mlp",
    )(x, pw["g_mlp"], pw["g_final"], pw["w_up"], pw["w_down"])


def _attention_bias():
    q = jnp.arange(ATTN_BLOCK)[None, :]
    k = jnp.arange(3 * ATTN_BLOCK)[:, None]
    dist = jnp.abs(k - ATTN_BLOCK - q).astype(F32)
    slopes = jnp.exp2(-8.0 * (jnp.arange(ATTN_HEADS, dtype=F32) + 1.0) / ATTN_HEADS)
    bias = jnp.where(dist <= WINDOW, -slopes[:, None, None] * dist * LOG2E, -jnp.inf)
    bias = bias.reshape(ATTN_HEADS // 2, 2, 3 * ATTN_BLOCK, ATTN_BLOCK)
    return jnp.concatenate([bias[:, 0], bias[:, 1]], axis=2)


def _split_kernel(w_ref, proj_ref, gbc_ref):
    n_main = C_GA
    proj_ref[:, :n_main] = w_ref[:, :n_main].astype(BF16)
    proj_ref[:, C_GA:C_GATE] = w_ref[:, n_main + N_GATES:n_main + N_GATES + D_MODEL].astype(BF16)
    gates = w_ref[:, n_main:n_main + N_GATES].astype(BF16)
    proj_ref[:, C_GATE:] = jnp.concatenate(
        [gates, jnp.zeros((gates.shape[0], LANES - N_GATES), BF16)], axis=1)
    gbc_ref[...] = w_ref[:, n_main + N_GATES + D_MODEL:].astype(BF16)


def _split_input_projection(w_in):
    depth, rows, n_in = w_in.shape
    blk = SPLIT_ROWS
    return pl.pallas_call(
        _split_kernel,
        grid=(depth, rows // blk),
        in_specs=[pl.BlockSpec((None, blk, n_in), lambda l, i: (l, i, 0))],
        out_specs=[pl.BlockSpec((None, blk, N_PROJ), lambda l, i: (l, i, 0)),
                   pl.BlockSpec((None, blk, 2 * D_MODEL), lambda l, i: (l, i, 0))],
        out_shape=[jax.ShapeDtypeStruct((depth, rows, N_PROJ), BF16),
                   jax.ShapeDtypeStruct((depth, rows, 2 * D_MODEL), BF16)],
        compiler_params=_params(),
        name="split_w_in",
    )(w_in)


def _prepare_params(w_in, conv_w, attn_sink, mlstm_gate_b, mlstm_norm_g, w_out_a, w_out_b, w_out_c,
                    w_o, norm_mix_g, norm_mlp_g, w_mlp_up, w_mlp_down, norm_final_g):
    depth = w_in.shape[0]
    w_proj, w_gbc = _split_input_projection(w_in)
    gate_b = jnp.concatenate([mlstm_gate_b, jnp.zeros((depth, LANES - N_GATES), F32)], axis=1)
    return {
        "w_proj": w_proj,
        "w_gbc": w_gbc,
        "w_vt": jnp.swapaxes(w_in[:, :, C_AV:C_AV + ATTN_KV_DIM], 1, 2).astype(BF16),
        "conv_w": conv_w,
        "sink": attn_sink,
        "gate_b": gate_b[:, None, :],
        "mnorm_g": mlstm_norm_g[:, None, :],
        "w_a": w_out_a.astype(BF16),
        "w_b": w_out_b.astype(BF16),
        "w_c": w_out_c.astype(BF16),
        "w_o": w_o.astype(BF16),
        "g_mix": norm_mix_g[:, None, :],
        "g_mlp": norm_mlp_g[:, None, :],
        "w_up": w_mlp_up.astype(BF16),
        "w_down": w_mlp_down.astype(BF16),
        "g_final": norm_final_g[None, :],
    }


def kernel(x_prompt, x_sample, w_in, conv_w, attn_sink, mlstm_gate_b, mlstm_norm_g, w_out_a,
           w_out_b, w_out_c, w_o, norm_mix_g, norm_mlp_g, w_mlp_up, w_mlp_down, norm_final_g):
    depth = w_in.shape[0]
    pw = _prepare_params(w_in, conv_w, attn_sink, mlstm_gate_b, mlstm_norm_g, w_out_a, w_out_b,
                         w_out_c, w_o, norm_mix_g, norm_mlp_g, w_mlp_up, w_mlp_down, norm_final_g)
    attn_bias = _attention_bias()

    def trunk(x):
        for layer in range(depth):
            x = _mix_out(x, _mix_in(x, pw, layer), pw, layer, attn_bias)
            x = _mlp(x, pw, layer, final_norm=(layer == depth - 1))
        return x

    return (trunk(x_prompt), trunk(x_sample))
```

```python
import functools

import jax
import jax.numpy as jnp
from jax import lax
from jax.experimental import pallas as pl
from jax.experimental.pallas import tpu as pltpu

F32 = jnp.float32
BF16 = jnp.bfloat16

LANES = 128
SUBLANES = 8
VMEM_LIMIT_BYTES = 56 * 1024 * 1024

D_MODEL = 1024
CONV_DIM = 512
ATTN_HEADS = 8
ATTN_KV_HEADS = 2
ATTN_GROUP = ATTN_HEADS // ATTN_KV_HEADS
ATTN_HEAD_DIM = 64
ATTN_DIM = ATTN_HEADS * ATTN_HEAD_DIM
ATTN_KV_DIM = ATTN_KV_HEADS * ATTN_HEAD_DIM
WINDOW = 128
ATTN_BLOCK = 128
MLSTM_HEADS = 4
MLSTM_HEAD_DIM = 128
MLSTM_DIM = MLSTM_HEADS * MLSTM_HEAD_DIM
MLSTM_CHUNK = 128
M_INIT = -1e30
D_FF = 4 * D_MODEL
RMS_EPS = 1e-6
LOG2E = 1.4426950408889634
N_GATES = 4 * MLSTM_HEADS

C_CONV = 0
C_AQ = C_CONV + 3 * CONV_DIM
C_AK = C_AQ + ATTN_DIM
C_AV = C_AK + ATTN_KV_DIM
C_ML = C_AV + ATTN_KV_DIM
C_GA = C_ML + 4 * MLSTM_DIM
C_GATE = C_GA + D_MODEL
SPLIT_COLS = 256
N_PROJ = C_GATE + SPLIT_COLS

TILE = 512
MLP_TILE = 1024
FF_CHUNK = 1024
GA_BLOCK = 256
MLSTM_PIECES_PER_PROJ = 6
MLSTM_PIECES_PER_ATTN = 4


def _dot(a, b):
    return jnp.dot(a, b, preferred_element_type=F32)


def _dot_nt(a, b):
    return lax.dot_general(a, b, (((1,), (1,)), ((), ())), preferred_element_type=F32)


def _rmsnorm(x, g):
    r = lax.rsqrt(jnp.mean(x * x, axis=-1, keepdims=True) + RMS_EPS)
    return (x * r) * g


def _log_sigmoid(x):
    return jnp.minimum(x, 0.0) - jnp.log(1.0 + jnp.exp(-jnp.abs(x)))


def _mlstm_tile(qkv_ref, gate_rows, st_ref, m_ref, reverse, emit):
    L = MLSTM_CHUNK
    dh = MLSTM_HEAD_DIM
    n_chunks = len(gate_rows)
    order = list(reversed(range(n_chunks))) if reverse else list(range(n_chunks))
    row = lax.broadcasted_iota(jnp.int32, (L, L), 0)
    col = lax.broadcasted_iota(jnp.int32, (L, L), 1)
    keep = (col >= row) if reverse else (col <= row)
    tri = jnp.where((row >= col) if reverse else (row <= col), 1.0, 0.0).astype(BF16)
    ones = jnp.ones((L, dh), BF16)

    gates_all = jnp.concatenate(gate_rows, axis=0)
    logf_all = _log_sigmoid(jnp.concatenate(
        [pltpu.roll(gates, MLSTM_HEADS, axis=0) for gates in gate_rows], axis=0))
    hi = logf_all.astype(BF16)
    rest = logf_all - hi.astype(F32)
    mid = rest.astype(BF16)
    lo = (rest - mid.astype(F32)).astype(BF16)
    b_terms = _dot(jnp.concatenate([hi, mid, lo], axis=0), tri)
    n_rows = n_chunks * SUBLANES
    b_all = b_terms[:n_rows] + b_terms[n_rows:2 * n_rows] + b_terms[2 * n_rows:]
    r_all = gates_all - b_all
    g_all = b_all[:, 0:1] if reverse else b_all[:, L - 1:L]
    w_all = g_all + r_all
    m_loc_all = jnp.max(w_all, axis=1, keepdims=True)
    e_all = jnp.exp(w_all - m_loc_all)

    stats = {}
    m_prev = m_ref[...]
    for c in order:
        rows8 = slice(c * SUBLANES, (c + 1) * SUBLANES)
        g, m_loc = g_all[rows8], m_loc_all[rows8]
        m_new = jnp.maximum(g + m_prev, m_loc)
        stats[c] = (e_all[rows8], m_prev, jnp.exp(g + m_prev - m_new), jnp.exp(m_loc - m_new))
        m_prev = m_new
    m_ref[...] = m_prev
    yield

    logf2_all = logf_all * LOG2E
    r2_all = r_all * LOG2E
    work = {}
    for c in order:
        e_state, m_in, _, _ = stats[c]
        rows8 = slice(c * SUBLANES, (c + 1) * SUBLANES)
        logf2, r2, m_in2 = logf2_all[rows8], r2_all[rows8], m_in * LOG2E
        rows = slice(c * L, (c + 1) * L)
        for h in range(MLSTM_HEADS):
            qh = qkv_ref[rows, h * dh:(h + 1) * dh]
            kh = qkv_ref[rows, MLSTM_DIM + h * dh:MLSTM_DIM + (h + 1) * dh]
            vh = qkv_ref[rows, 2 * MLSTM_DIM + h * dh:2 * MLSTM_DIM + (h + 1) * dh]
            rm = jnp.where(keep, r2[h:h + 1, :], -jnp.inf)
            b_t = jnp.sum(jnp.where(keep, logf2[h:h + 1, :], 0.0), axis=1, keepdims=True)
            mp = m_in2[h:h + 1, 0:1]
            c_t = jnp.maximum(jnp.max(rm, axis=1, keepdims=True), mp)
            p = jnp.exp2(rm - c_t) * _dot_nt(qh, kh)
            a_t = jnp.exp2(mp - c_t)
            lhs = jnp.concatenate([p.astype(BF16), (a_t * qh.astype(F32)).astype(BF16)], axis=1)
            v1 = jnp.concatenate([vh, ones], axis=1)
            ke_t = (jnp.transpose(kh.astype(F32)) * e_state[h:h + 1, :]).astype(BF16)
            work[c, h] = (lhs, v1, jnp.exp2(-(b_t + c_t)), _dot(ke_t, v1))
            yield

    state = [st_ref[h] for h in range(MLSTM_HEADS)]
    for c in order:
        _, _, a_vec, c_vec = stats[c]
        for h in range(MLSTM_HEADS):
            lhs, v1, bound, upd = work[c, h]
            tot = _dot(lhs, jnp.concatenate([v1, state[h].astype(BF16)], axis=0))
            emit(c, h, tot[:, :dh] / jnp.maximum(jnp.abs(tot[:, dh:]), bound))
            state[h] = a_vec[h:h + 1, 0:1] * state[h] + c_vec[h:h + 1, 0:1] * upd
            yield
    for h in range(MLSTM_HEADS):
        st_ref[h] = state[h]


def _init_state(st_ref, m_ref):
    st_ref[...] = jnp.zeros(st_ref.shape, F32)
    m_ref[...] = jnp.full(m_ref.shape, M_INIT, F32)


def _mix_in_kernel(x_ref, xp_ref, xn_ref, gmix_ref, w_ref, wvt_ref, convw_ref, gateb_ref, wa_ref,
                   hn_ref, pa_ref, aq_ref, kd_ref, vt_ref, mqkv_ref, smo_ref, grow_ref, hf_ref,
                   st_scr, m_scr, *, tile):
    i = pl.program_id(1)
    n_tiles = pl.num_programs(1)

    @pl.when(i == 0)
    def _():
        _init_state(st_scr, m_scr)

    g_mix = gmix_ref[...]
    h = _rmsnorm(x_ref[...], g_mix).astype(BF16)
    hn_ref[...] = h

    def attn_q():
        aq_ref[...] = (_dot(h, w_ref[:, C_AQ:C_AK]) * (ATTN_HEAD_DIM ** -0.5 * LOG2E)).astype(BF16)

    def attn_kv():
        uk = _dot(h, w_ref[:, C_AK:C_AK + ATTN_KV_DIM])
        lane = lax.broadcasted_iota(jnp.int32, (tile, LANES), 1)
        low = lane < ATTN_HEAD_DIM
        swapped = pltpu.roll(uk, ATTN_HEAD_DIM, axis=1)
        kd_ref[:, :LANES] = jnp.where(low, uk, swapped).astype(BF16)
        kd_ref[:, LANES:] = jnp.where(low, swapped, uk).astype(BF16)
        vt_ref[...] = _dot_nt(wvt_ref[...], h).astype(BF16)

    ul = _dot(h, w_ref[:, C_ML:C_ML + 4 * MLSTM_DIM])
    gcol = _dot(h, w_ref[:, C_GATE:C_GATE + LANES]) + gateb_ref[...]
    uc = _dot(h, w_ref[:, C_CONV:C_CONV + 3 * CONV_DIM])
    attn_q()

    mqkv_ref[:, :MLSTM_DIM] = ul[:, :MLSTM_DIM].astype(BF16)
    mqkv_ref[:, MLSTM_DIM:2 * MLSTM_DIM] = (
        ul[:, MLSTM_DIM:2 * MLSTM_DIM] * (MLSTM_HEAD_DIM ** -0.5)).astype(BF16)
    mqkv_ref[:, 2 * MLSTM_DIM:] = ul[:, 2 * MLSTM_DIM:3 * MLSTM_DIM].astype(BF16)
    smo_ref[...] = jax.nn.sigmoid(ul[:, 3 * MLSTM_DIM:]).astype(BF16)
    gate_rows = []
    for c in range(tile // MLSTM_CHUNK):
        rows = slice(c * MLSTM_CHUNK, (c + 1) * MLSTM_CHUNK)
        g_rows = jnp.transpose(gcol[rows, :])[:2 * SUBLANES, :]
        grow_ref[:, rows] = g_rows
        gate_rows.append(g_rows[:SUBLANES, :])

    def emit_fwd(c, hd, out):
        hf_ref[c * MLSTM_CHUNK:(c + 1) * MLSTM_CHUNK, hd * MLSTM_HEAD_DIM:(hd + 1) * MLSTM_HEAD_DIM] = out

    mlstm = _mlstm_tile(mqkv_ref, gate_rows, st_scr, m_scr, False, emit_fwd)

    def conv_out():
        halo = jnp.concatenate([_rmsnorm(xp_ref[...], g_mix), _rmsnorm(xn_ref[...], g_mix)], axis=0)
        uh = _dot(halo.astype(BF16), w_ref[:, C_CONV + CONV_DIM:C_CONV + 3 * CONV_DIM])
        zh = uh[:, :CONV_DIM] * uh[:, CONV_DIM:]
        z_prev = jnp.where(i > 0, zh[SUBLANES - 1:SUBLANES, :], 0.0)
        z_next = jnp.where(i < n_tiles - 1, zh[SUBLANES:SUBLANES + 1, :], 0.0)
        z = uc[:, CONV_DIM:2 * CONV_DIM] * uc[:, 2 * CONV_DIM:]
        trow = lax.broadcasted_iota(jnp.int32, (tile, CONV_DIM), 0)
        z_dn = jnp.where(trow == 0, z_prev, pltpu.roll(z, 1, axis=0))
        z_up = jnp.where(trow == tile - 1, z_next, pltpu.roll(z, tile - 1, axis=0))
        conv = convw_ref[0:1, :] * z_dn + convw_ref[1:2, :] * z + convw_ref[2:3, :] * z_up
        return _dot((uc[:, :CONV_DIM] * conv).astype(BF16), wa_ref[...])

    def ga_block(y_a, blk):
        cols = slice(blk * GA_BLOCK, (blk + 1) * GA_BLOCK)
        sg = jax.nn.sigmoid(_dot(h, w_ref[:, C_GA + blk * GA_BLOCK:C_GA + (blk + 1) * GA_BLOCK]))
        pa_ref[:, cols] = (sg * y_a[:, cols]).astype(BF16)

    def advance():
        for _ in range(MLSTM_PIECES_PER_PROJ):
            next(mlstm, None)

    advance()
    y_a = conv_out()
    advance()
    n_ga = D_MODEL // GA_BLOCK
    for blk in range(n_ga):
        ga_block(y_a, blk)
        advance()
    attn_kv()
    for _ in mlstm:
        pass


def _mix_out_kernel(sink_ref, x_ref, hn_ref, pa_ref, aq_ref, kd_ref, kdp_ref, kdn_ref,
                    vt_ref, vtp_ref, vtn_ref, mqkv_ref, smo_ref, grow_ref, hf_ref, bias_ref,
                    mnorm_ref, wg_ref, wb_ref, wc_ref, wo_ref,
                    xo_ref,
                    kwin, vtwin, att_scr, yc_scr, sg_scr, st_scr, m_scr, *, tile):
    i = pl.program_id(1)
    n_tiles = pl.num_programs(1)
    ti = n_tiles - 1 - i
    n_qblk = tile // ATTN_BLOCK

    @pl.when(i == 0)
    def _():
        _init_state(st_scr, m_scr)

    kwin[0:ATTN_BLOCK, :] = kdp_ref[...]
    kwin[ATTN_BLOCK:ATTN_BLOCK + tile, :] = kd_ref[...]
    kwin[ATTN_BLOCK + tile:, :] = kdn_ref[...]
    vtwin[:, 0:ATTN_BLOCK] = vtp_ref[...]
    vtwin[:, ATTN_BLOCK:ATTN_BLOCK + tile] = vt_ref[...]
    vtwin[:, ATTN_BLOCK + tile:] = vtn_ref[...]

    h = hn_ref[...]
    lane = lax.broadcasted_iota(jnp.int32, (ATTN_BLOCK, LANES), 1)
    low = lane < ATTN_HEAD_DIM
    krow = lax.broadcasted_iota(jnp.int32, (3 * ATTN_BLOCK, 2 * ATTN_BLOCK), 0)
    second = lax.broadcasted_iota(jnp.int32, (1, 2 * ATTN_BLOCK), 1) >= ATTN_BLOCK

    gate_cols = 2 * D_MODEL // (n_qblk * ATTN_KV_HEADS)

    def gate_block(blk):
        cols = slice(blk * gate_cols, (blk + 1) * gate_cols)
        sg_scr[:, cols] = jax.nn.sigmoid(_dot(h, wg_ref[:, cols])).astype(BF16)

    def attn_scores(j, g):
        rows = slice(j * ATTN_BLOCK, (j + 1) * ATTN_BLOCK)
        wrows = slice(j * ATTN_BLOCK, (j + 3) * ATTN_BLOCK)
        outside = None
        if j == 0:
            outside = jnp.logical_and(krow < ATTN_BLOCK, ti == 0)
        if j == n_qblk - 1:
            after = jnp.logical_and(krow >= 2 * ATTN_BLOCK, ti == n_tiles - 1)
            outside = after if outside is None else jnp.logical_or(outside, after)
        kw = kwin[wrows, g * LANES:(g + 1) * LANES]
        scores = []
        for pr in range(2):
            pair = g * 2 + pr
            qp = aq_ref[rows, pair * LANES:(pair + 1) * LANES]
            zero = jnp.zeros_like(qp)
            q_both = jnp.concatenate([jnp.where(low, qp, zero), jnp.where(low, zero, qp)], axis=0)
            s = _dot_nt(kw, q_both) + bias_ref[pair]
            if outside is not None:
                s = jnp.where(outside, -jnp.inf, s)
            scores.append(s)
        gate_block(j * ATTN_KV_HEADS + g)
        return scores

    def attn_finish(j, g, scores):
        rows = slice(j * ATTN_BLOCK, (j + 1) * ATTN_BLOCK)
        wrows = slice(j * ATTN_BLOCK, (j + 3) * ATTN_BLOCK)
        vt = vtwin[g * ATTN_HEAD_DIM:(g + 1) * ATTN_HEAD_DIM, wrows]
        for pr, s in enumerate(scores):
            pair = g * 2 + pr
            sk = jnp.where(second, sink_ref[2 * pair + 1] * LOG2E, sink_ref[2 * pair] * LOG2E)
            m = jnp.maximum(jnp.max(s, axis=0, keepdims=True), sk)
            p = jnp.exp2(s - m)
            denom = jnp.sum(p, axis=0, keepdims=True) + jnp.exp2(sk - m)
            o_t = _dot(vt, p.astype(BF16)) / denom
            o_pair = jnp.concatenate([o_t[:, :ATTN_BLOCK], o_t[:, ATTN_BLOCK:]], axis=0)
            att_scr[rows, pair * LANES:(pair + 1) * LANES] = jnp.transpose(o_pair).astype(BF16)

    def emit_bwd(c, hd, out):
        rows = slice(c * MLSTM_CHUNK, (c + 1) * MLSTM_CHUNK)
        sl = slice(hd * MLSTM_HEAD_DIM, (hd + 1) * MLSTM_HEAD_DIM)
        hm = hf_ref[rows, sl] + out
        hm = hm * lax.rsqrt(jnp.mean(hm * hm, axis=-1, keepdims=True) + RMS_EPS)
        yc_scr[rows, sl] = (smo_ref[rows, sl].astype(F32) * (hm * mnorm_ref[:, sl])).astype(BF16)

    gate_rows = [grow_ref[SUBLANES:, c * MLSTM_CHUNK:(c + 1) * MLSTM_CHUNK]
                 for c in range(tile // MLSTM_CHUNK)]
    mlstm = _mlstm_tile(mqkv_ref, gate_rows, st_scr, m_scr, True, emit_bwd)

    units = [(j, g) for j in range(n_qblk) for g in range(ATTN_KV_HEADS)]
    scores = attn_scores(*units[0])
    for k, unit in enumerate(units):
        following = attn_scores(*units[k + 1]) if k + 1 < len(units) else None
        attn_finish(*unit, scores)
        for _ in range(MLSTM_PIECES_PER_ATTN):
            next(mlstm, None)
        scores = following
    next(mlstm, None)
    y_b = _dot(att_scr[...], wb_ref[...]).astype(BF16)
    for _ in mlstm:
        pass

    y_c = _dot(yc_scr[...], wc_ref[...]).astype(BF16)
    merged = pa_ref[...] + sg_scr[:, :D_MODEL] * y_b + sg_scr[:, D_MODEL:] * y_c
    xo_ref[...] = x_ref[...] + _dot(merged, wo_ref[...])


def _mlp_kernel(x_ref, g_ref, gf_ref, wup_ref, wdn_ref, o_ref, *, final_norm):
    x = x_ref[...]
    h = _rmsnorm(x, g_ref[...]).astype(BF16)
    acc = x
    for c in range(D_FF // FF_CHUNK):
        cols = slice(c * FF_CHUNK, (c + 1) * FF_CHUNK)
        up = jnp.maximum(_dot(h, wup_ref[:, cols]), 0.0)
        acc = acc + _dot((up * up).astype(BF16), wdn_ref[cols, :])
    if final_norm:
        acc = _rmsnorm(acc, gf_ref[...])
    o_ref[...] = acc


def _resident(shape):
    nd = len(shape)
    return pl.BlockSpec(shape, lambda b, i: (0,) * nd, pipeline_mode=pl.Buffered(1))


def _layer_resident(shape, layer):
    nd = len(shape)
    return pl.BlockSpec((None,) + shape, lambda b, i: (layer,) + (0,) * nd,
                        pipeline_mode=pl.Buffered(1))


def _params():
    return pltpu.CompilerParams(dimension_semantics=("arbitrary", "arbitrary"),
                                vmem_limit_bytes=VMEM_LIMIT_BYTES)


def _mlstm_state_scratch():
    return [pltpu.VMEM((MLSTM_HEADS, MLSTM_HEAD_DIM, 2 * MLSTM_HEAD_DIM), F32),
            pltpu.VMEM((SUBLANES, LANES), F32)]


def _mix_in(x, pw, layer):
    bsz, seq, _ = x.shape
    tile = TILE
    n_tiles = seq // tile
    halo_per_tile = tile // SUBLANES
    n_halo = seq // SUBLANES

    def tok(width):
        return pl.BlockSpec((None, tile, width), lambda b, i: (b, i, 0))

    in_specs = [
        tok(D_MODEL),
        pl.BlockSpec((None, SUBLANES, D_MODEL),
                     lambda b, i: (b, jnp.maximum(i * halo_per_tile - 1, 0), 0)),
        pl.BlockSpec((None, SUBLANES, D_MODEL),
                     lambda b, i: (b, jnp.minimum((i + 1) * halo_per_tile, n_halo - 1), 0)),
        _layer_resident((1, D_MODEL), layer),
        _layer_resident((D_MODEL, N_PROJ), layer),
        _layer_resident((ATTN_KV_DIM, D_MODEL), layer),
        _layer_resident((3, CONV_DIM), layer),
        _layer_resident((1, LANES), layer),
        _layer_resident((CONV_DIM, D_MODEL), layer),
    ]
    def tok_t(rows):
        return pl.BlockSpec((None, rows, tile), lambda b, i: (b, 0, i))

    def shape(width, dtype=BF16):
        return jax.ShapeDtypeStruct((bsz, seq, width), dtype)

    def shape_t(rows, dtype):
        return jax.ShapeDtypeStruct((bsz, rows, seq), dtype)

    out_shape = [shape(D_MODEL), shape(D_MODEL), shape(ATTN_DIM), shape(2 * LANES),
                 shape_t(ATTN_KV_DIM, BF16), shape(3 * MLSTM_DIM), shape(MLSTM_DIM),
                 shape_t(N_GATES, F32), shape(MLSTM_DIM, F32)]
    out_specs = [tok(D_MODEL), tok(D_MODEL), tok(ATTN_DIM), tok(2 * LANES),
                 tok_t(ATTN_KV_DIM), tok(3 * MLSTM_DIM), tok(MLSTM_DIM),
                 tok_t(N_GATES), tok(MLSTM_DIM)]
    return pl.pallas_call(
        functools.partial(_mix_in_kernel, tile=tile),
        grid=(bsz, n_tiles),
        in_specs=in_specs,
        out_specs=out_specs,
        out_shape=out_shape,
        scratch_shapes=_mlstm_state_scratch(),
        compiler_params=_params(),
        name="mix_in",
    )(x, x, x, pw["g_mix"], pw["w_proj"], pw["w_vt"], pw["conv_w"], pw["gate_b"], pw["w_a"])


def _mix_out(x, mid, pw, layer, attn_bias):
    hn, pa, aq, kd, vt, mqkv, smo, grow, hf = mid
    bsz, seq, _ = x.shape
    tile = TILE
    n_tiles = seq // tile
    blk_per_tile = tile // ATTN_BLOCK
    n_blk = seq // ATTN_BLOCK

    def tok(width):
        return pl.BlockSpec((None, tile, width), lambda b, i: (b, n_tiles - 1 - i, 0))

    def blk_prev(width):
        return pl.BlockSpec(
            (None, ATTN_BLOCK, width),
            lambda b, i: (b, jnp.maximum((n_tiles - 1 - i) * blk_per_tile - 1, 0), 0))

    def blk_next(width):
        return pl.BlockSpec(
            (None, ATTN_BLOCK, width),
            lambda b, i: (b, jnp.minimum((n_tiles - i) * blk_per_tile, n_blk - 1), 0))

    in_specs = [
        pl.BlockSpec(memory_space=pltpu.SMEM),
        tok(D_MODEL), tok(D_MODEL), tok(D_MODEL), tok(ATTN_DIM),
        tok(2 * LANES), blk_prev(2 * LANES), blk_next(2 * LANES),
        pl.BlockSpec((None, ATTN_KV_DIM, tile), lambda b, i: (b, 0, n_tiles - 1 - i)),
        pl.BlockSpec((None, ATTN_KV_DIM, ATTN_BLOCK),
                     lambda b, i: (b, 0, jnp.maximum((n_tiles - 1 - i) * blk_per_tile - 1, 0))),
        pl.BlockSpec((None, ATTN_KV_DIM, ATTN_BLOCK),
                     lambda b, i: (b, 0, jnp.minimum((n_tiles - i) * blk_per_tile, n_blk - 1))),
        tok(3 * MLSTM_DIM), tok(MLSTM_DIM),
        pl.BlockSpec((None, 2 * SUBLANES, tile), lambda b, i: (b, 0, n_tiles - 1 - i)),
        tok(MLSTM_DIM),
        _resident((ATTN_HEADS // 2, 3 * ATTN_BLOCK, 2 * ATTN_BLOCK)),
        _layer_resident((1, MLSTM_DIM), layer),
        _layer_resident((D_MODEL, 2 * D_MODEL), layer),
        _layer_resident((ATTN_DIM, D_MODEL), layer),
        _layer_resident((MLSTM_DIM, D_MODEL), layer),
        _layer_resident((D_MODEL, D_MODEL), layer),
    ]
    scratch = [pltpu.VMEM((tile + 2 * ATTN_BLOCK, 2 * LANES), BF16),
               pltpu.VMEM((ATTN_KV_DIM, tile + 2 * ATTN_BLOCK), BF16),
               pltpu.VMEM((tile, ATTN_DIM), BF16),
               pltpu.VMEM((tile, MLSTM_DIM), BF16),
               pltpu.VMEM((tile, 2 * D_MODEL), BF16)]
    return pl.pallas_call(
        functools.partial(_mix_out_kernel, tile=tile),
        grid=(bsz, n_tiles),
        in_specs=in_specs,
        out_specs=tok(D_MODEL),
        out_shape=jax.ShapeDtypeStruct((bsz, seq, D_MODEL), F32),
        scratch_shapes=scratch + _mlstm_state_scratch(),
        compiler_params=_params(),
        name="mix_out",
    )(pw["sink"][layer], x, hn, pa, aq, kd, kd, kd, vt, vt, vt, mqkv, smo, grow, hf,
      attn_bias, pw["mnorm_g"], pw["w_gbc"], pw["w_b"], pw["w_c"], pw["w_o"])


def _mlp(x, pw, layer, final_norm):
    bsz, seq, _ = x.shape
    tile = MLP_TILE
    tok = pl.BlockSpec((None, tile, D_MODEL), lambda b, i: (b, i, 0))
    return pl.pallas_call(
        functools.partial(_mlp_kernel, final_norm=final_norm),
        grid=(bsz, seq // tile),
        in_specs=[tok, _layer_resident((1, D_MODEL), layer), _resident((1, D_MODEL)),
                  _layer_resident((D_MODEL, D_FF), layer), _layer_resident((D_FF, D_MODEL), layer)],
        out_specs=tok,
        out_shape=jax.ShapeDtypeStruct((bsz, seq, D_MODEL), F32),
        compiler_params=_params(),
        name="mlp",
    )(x, pw["g_mlp"], pw["g_final"], pw["w_up"], pw["w_down"])


def _attention_bias():
    q = jnp.arange(ATTN_BLOCK)[None, :]
    k = jnp.arange(3 * ATTN_BLOCK)[:, None]
    dist = jnp.abs(k - ATTN_BLOCK - q).astype(F32)
    slopes = jnp.exp2(-8.0 * (jnp.arange(ATTN_HEADS, dtype=F32) + 1.0) / ATTN_HEADS)
    bias = jnp.where(dist <= WINDOW, -slopes[:, None, None] * dist * LOG2E, -jnp.inf)
    bias = bias.reshape(ATTN_HEADS // 2, 2, 3 * ATTN_BLOCK, ATTN_BLOCK)
    return jnp.concatenate([bias[:, 0], bias[:, 1]], axis=2)


def _split_kernel(a_ref, b_ref, o_ref, *, first_shifted, end_shifted, gate_block):
    j = pl.program_id(1)
    a = a_ref[...]
    shifted = jnp.concatenate([a[N_GATES:], b_ref[...]], axis=0)
    src = jnp.where(jnp.logical_and(j >= first_shifted, j < end_shifted), shifted, a)
    if gate_block is not None:
        row = lax.broadcasted_iota(jnp.int32, src.shape, 0)
        src = jnp.where(jnp.logical_and(j == gate_block, row >= N_GATES), 0.0, src)
    o_ref[...] = jnp.transpose(src).astype(BF16)


def _split_input_projection(w_in):
    depth, d_model, n_in = w_in.shape
    wt = jnp.swapaxes(w_in, 1, 2)
    blk = SPLIT_COLS
    assert n_in % N_GATES == 0
    ga_blk = C_GA // blk
    gate_blk = C_GATE // blk
    gbc_blk = (C_GA + D_MODEL) // blk
    assert C_GA % blk == 0 and C_GATE % blk == 0 and N_PROJ == C_GATE + blk

    def call(n_blocks, a_index, kernel_kwargs, name):
        return pl.pallas_call(
            functools.partial(_split_kernel, **kernel_kwargs),
            grid=(depth, n_blocks),
            in_specs=[pl.BlockSpec((None, blk, d_model), lambda l, j: (l, a_index(j), 0)),
                      pl.BlockSpec((None, N_GATES, d_model),
                                   lambda l, j: (l, jnp.minimum((a_index(j) + 1) * (blk // N_GATES),
                                                                n_in // N_GATES - 1), 0))],
            out_specs=pl.BlockSpec((None, d_model, blk), lambda l, j: (l, 0, j)),
            out_shape=jax.ShapeDtypeStruct((depth, d_model, n_blocks * blk), BF16),
            compiler_params=_params(),
            name=name,
        )(wt, wt)

    w_proj = call(N_PROJ // blk, lambda j: jnp.where(j == gate_blk, ga_blk, j),
                  dict(first_shifted=ga_blk, end_shifted=gate_blk, gate_block=gate_blk), "split_proj")
    w_gbc = call(2 * D_MODEL // blk, lambda j: j + gbc_blk,
                 dict(first_shifted=0, end_shifted=2 * D_MODEL // blk, gate_block=None), "split_gbc")
    return w_proj, w_gbc


def _prepare_params(w_in, conv_w, attn_sink, mlstm_gate_b, mlstm_norm_g, w_out_a, w_out_b, w_out_c,
                    w_o, norm_mix_g, norm_mlp_g, w_mlp_up, w_mlp_down, norm_final_g):
    depth = w_in.shape[0]
    w_proj, w_gbc = _split_input_projection(w_in)
    gate_b = jnp.concatenate([mlstm_gate_b, jnp.zeros((depth, LANES - N_GATES), F32)], axis=1)
    return {
        "w_proj": w_proj,
        "w_gbc": w_gbc,
        "w_vt": jnp.swapaxes(w_in[:, :, C_AV:C_AV + ATTN_KV_DIM], 1, 2).astype(BF16),
        "conv_w": conv_w,
        "sink": attn_sink,
        "gate_b": gate_b[:, None, :],
        "mnorm_g": mlstm_norm_g[:, None, :],
        "w_a": w_out_a.astype(BF16),
        "w_b": w_out_b.astype(BF16),
        "w_c": w_out_c.astype(BF16),
        "w_o": w_o.astype(BF16),
        "g_mix": norm_mix_g[:, None, :],
        "g_mlp": norm_mlp_g[:, None, :],
        "w_up": w_mlp_up.astype(BF16),
        "w_down": w_mlp_down.astype(BF16),
        "g_final": norm_final_g[None, :],
    }


def kernel(x_prompt, x_sample, w_in, conv_w, attn_sink, mlstm_gate_b, mlstm_norm_g, w_out_a,
           w_out_b, w_out_c, w_o, norm_mix_g, norm_mlp_g, w_mlp_up, w_mlp_down, norm_final_g):
    depth = w_in.shape[0]
    pw = _prepare_params(w_in, conv_w, attn_sink, mlstm_gate_b, mlstm_norm_g, w_out_a, w_out_b,
                         w_out_c, w_o, norm_mix_g, norm_mlp_g, w_mlp_up, w_mlp_down, norm_final_g)
    attn_bias = _attention_bias()

    def trunk(x):
        for layer in range(depth):
            x = _mix_out(x, _mix_in(x, pw, layer), pw, layer, attn_bias)
            x = _mlp(x, pw, layer, final_norm=(layer == depth - 1))
        return x

    return (trunk(x_prompt), trunk(x_sample))
```

```python
import functools

import jax
import jax.numpy as jnp
from jax import lax
from jax.experimental import pallas as pl
from jax.experimental.pallas import tpu as pltpu

F32 = jnp.float32
BF16 = jnp.bfloat16

LANES = 128
SUBLANES = 8
VMEM_LIMIT_BYTES = 56 * 1024 * 1024

D_MODEL = 1024
CONV_DIM = 512
ATTN_HEADS = 8
ATTN_KV_HEADS = 2
ATTN_GROUP = ATTN_HEADS // ATTN_KV_HEADS
ATTN_HEAD_DIM = 64
ATTN_DIM = ATTN_HEADS * ATTN_HEAD_DIM
ATTN_KV_DIM = ATTN_KV_HEADS * ATTN_HEAD_DIM
WINDOW = 128
ATTN_BLOCK = 128
MLSTM_HEADS = 4
MLSTM_HEAD_DIM = 128
MLSTM_DIM = MLSTM_HEADS * MLSTM_HEAD_DIM
MLSTM_CHUNK = 128
M_INIT = -1e30
D_FF = 4 * D_MODEL
RMS_EPS = 1e-6
LOG2E = 1.4426950408889634
N_GATES = 4 * MLSTM_HEADS

C_CONV = 0
C_AQ = C_CONV + 3 * CONV_DIM
C_AK = C_AQ + ATTN_DIM
C_AV = C_AK + ATTN_KV_DIM
C_ML = C_AV + ATTN_KV_DIM
C_GA = C_ML + 4 * MLSTM_DIM
C_GATE = C_GA + D_MODEL
SPLIT_COLS = 256
N_PROJ = C_GATE + SPLIT_COLS

TILE = 512
MLP_TILE = 1024
FF_CHUNK = 1024
GA_BLOCK = 256
MLSTM_PIECES_PER_PROJ = 6
MLSTM_PIECES_PER_ATTN = 4


def _dot(a, b):
    return jnp.dot(a, b, preferred_element_type=F32)


def _dot_nt(a, b):
    return lax.dot_general(a, b, (((1,), (1,)), ((), ())), preferred_element_type=F32)


def _rmsnorm(x, g):
    r = lax.rsqrt(jnp.mean(x * x, axis=-1, keepdims=True) + RMS_EPS)
    return (x * r) * g


def _log_sigmoid(x):
    return jnp.minimum(x, 0.0) - jnp.log(1.0 + jnp.exp(-jnp.abs(x)))


def _mlstm_tile(qkv_ref, gate_rows, st_ref, m_ref, reverse, emit):
    L = MLSTM_CHUNK
    dh = MLSTM_HEAD_DIM
    n_chunks = len(gate_rows)
    order = list(reversed(range(n_chunks))) if reverse else list(range(n_chunks))
    row = lax.broadcasted_iota(jnp.int32, (L, L), 0)
    col = lax.broadcasted_iota(jnp.int32, (L, L), 1)
    keep = (col >= row) if reverse else (col <= row)
    tri = jnp.where((row >= col) if reverse else (row <= col), 1.0, 0.0).astype(BF16)
    ones = jnp.ones((L, dh), BF16)

    gates_all = jnp.concatenate(gate_rows, axis=0)
    logf_all = _log_sigmoid(jnp.concatenate(
        [pltpu.roll(gates, MLSTM_HEADS, axis=0) for gates in gate_rows], axis=0))
    hi = logf_all.astype(BF16)
    rest = logf_all - hi.astype(F32)
    mid = rest.astype(BF16)
    lo = (rest - mid.astype(F32)).astype(BF16)
    b_terms = _dot(jnp.concatenate([hi, mid, lo], axis=0), tri)
    n_rows = n_chunks * SUBLANES
    b_all = b_terms[:n_rows] + b_terms[n_rows:2 * n_rows] + b_terms[2 * n_rows:]
    r_all = gates_all - b_all
    g_all = b_all[:, 0:1] if reverse else b_all[:, L - 1:L]
    w_all = g_all + r_all
    m_loc_all = jnp.max(w_all, axis=1, keepdims=True)
    e_all = jnp.exp(w_all - m_loc_all)

    stats = {}
    m_prev = m_ref[...]
    for c in order:
        rows8 = slice(c * SUBLANES, (c + 1) * SUBLANES)
        g, m_loc = g_all[rows8], m_loc_all[rows8]
        m_new = jnp.maximum(g + m_prev, m_loc)
        stats[c] = (e_all[rows8], m_prev, jnp.exp(g + m_prev - m_new), jnp.exp(m_loc - m_new))
        m_prev = m_new
    m_ref[...] = m_prev
    yield

    logf2_all = logf_all * LOG2E
    r2_all = r_all * LOG2E
    work = {}
    for c in order:
        e_state, m_in, _, _ = stats[c]
        rows8 = slice(c * SUBLANES, (c + 1) * SUBLANES)
        logf2, r2, m_in2 = logf2_all[rows8], r2_all[rows8], m_in * LOG2E
        rows = slice(c * L, (c + 1) * L)
        for h in range(MLSTM_HEADS):
            qh = qkv_ref[rows, h * dh:(h + 1) * dh]
            kh = qkv_ref[rows, MLSTM_DIM + h * dh:MLSTM_DIM + (h + 1) * dh]
            vh = qkv_ref[rows, 2 * MLSTM_DIM + h * dh:2 * MLSTM_DIM + (h + 1) * dh]
            rm = jnp.where(keep, r2[h:h + 1, :], -jnp.inf)
            b_t = jnp.sum(jnp.where(keep, logf2[h:h + 1, :], 0.0), axis=1, keepdims=True)
            mp = m_in2[h:h + 1, 0:1]
            c_t = jnp.maximum(jnp.max(rm, axis=1, keepdims=True), mp)
            p = jnp.exp2(rm - c_t) * _dot_nt(qh, kh)
            a_t = jnp.exp2(mp - c_t)
            lhs = jnp.concatenate([p.astype(BF16), (a_t * qh.astype(F32)).astype(BF16)], axis=1)
            v1 = jnp.concatenate([vh, ones], axis=1)
            ke_t = (jnp.transpose(kh.astype(F32)) * e_state[h:h + 1, :]).astype(BF16)
            work[c, h] = (lhs, v1, jnp.exp2(-(b_t + c_t)), _dot(ke_t, v1))
            yield

    state = [st_ref[h] for h in range(MLSTM_HEADS)]
    for c in order:
        _, _, a_vec, c_vec = stats[c]
        for h in range(MLSTM_HEADS):
            lhs, v1, bound, upd = work[c, h]
            tot = _dot(lhs, jnp.concatenate([v1, state[h].astype(BF16)], axis=0))
            emit(c, h, tot[:, :dh] / jnp.maximum(jnp.abs(tot[:, dh:]), bound))
            state[h] = a_vec[h:h + 1, 0:1] * state[h] + c_vec[h:h + 1, 0:1] * upd
            yield
    for h in range(MLSTM_HEADS):
        st_ref[h] = state[h]


def _init_state(st_ref, m_ref):
    st_ref[...] = jnp.zeros(st_ref.shape, F32)
    m_ref[...] = jnp.full(m_ref.shape, M_INIT, F32)


def _mix_in_kernel(x_ref, xp_ref, xn_ref, gmix_ref, w_ref, wvt_ref, convw_ref, gateb_ref, wa_ref,
                   hn_ref, pa_ref, aq_ref, kd_ref, vt_ref, mqkv_ref, smo_ref, grow_ref, hf_ref,
                   st_scr, m_scr, *, tile):
    i = pl.program_id(1)
    n_tiles = pl.num_programs(1)

    @pl.when(i == 0)
    def _():
        _init_state(st_scr, m_scr)

    g_mix = gmix_ref[...]
    h = _rmsnorm(x_ref[...], g_mix).astype(BF16)
    hn_ref[...] = h

    def attn_q():
        aq_ref[...] = (_dot(h, w_ref[:, C_AQ:C_AK]) * (ATTN_HEAD_DIM ** -0.5 * LOG2E)).astype(BF16)

    def attn_kv():
        lane = lax.broadcasted_iota(jnp.int32, (tile, LANES), 1)
        low = lane < ATTN_HEAD_DIM
        swapped = pltpu.roll(uk, ATTN_HEAD_DIM, axis=1)
        kd_ref[:, :LANES] = jnp.where(low, uk, swapped).astype(BF16)
        kd_ref[:, LANES:] = jnp.where(low, swapped, uk).astype(BF16)
        vt_ref[...] = _dot_nt(wvt_ref[...], h).astype(BF16)

    ul = _dot(h, w_ref[:, C_ML:C_ML + 4 * MLSTM_DIM])
    ukg = _dot(h, jnp.concatenate([w_ref[:, C_AK:C_AK + ATTN_KV_DIM],
                                   w_ref[:, C_GATE:C_GATE + LANES]], axis=1))
    uk = ukg[:, :ATTN_KV_DIM]
    gcol = ukg[:, ATTN_KV_DIM:] + gateb_ref[...]
    halo = jnp.concatenate([_rmsnorm(xp_ref[...], g_mix), _rmsnorm(xn_ref[...], g_mix)], axis=0)
    uc_all = _dot(jnp.concatenate([h, halo.astype(BF16)], axis=0),
                  w_ref[:, C_CONV:C_CONV + 3 * CONV_DIM])
    uc, uh = uc_all[:tile], uc_all[tile:]
    attn_q()

    mqkv_ref[:, :MLSTM_DIM] = ul[:, :MLSTM_DIM].astype(BF16)
    mqkv_ref[:, MLSTM_DIM:2 * MLSTM_DIM] = (
        ul[:, MLSTM_DIM:2 * MLSTM_DIM] * (MLSTM_HEAD_DIM ** -0.5)).astype(BF16)
    mqkv_ref[:, 2 * MLSTM_DIM:] = ul[:, 2 * MLSTM_DIM:3 * MLSTM_DIM].astype(BF16)
    smo_ref[...] = jax.nn.sigmoid(ul[:, 3 * MLSTM_DIM:]).astype(BF16)
    gate_rows = []
    for c in range(tile // MLSTM_CHUNK):
        rows = slice(c * MLSTM_CHUNK, (c + 1) * MLSTM_CHUNK)
        g_rows = jnp.transpose(gcol[rows, :])[:2 * SUBLANES, :]
        grow_ref[:, rows] = g_rows
        gate_rows.append(g_rows[:SUBLANES, :])

    def emit_fwd(c, hd, out):
        hf_ref[c * MLSTM_CHUNK:(c + 1) * MLSTM_CHUNK, hd * MLSTM_HEAD_DIM:(hd + 1) * MLSTM_HEAD_DIM] = out

    mlstm = _mlstm_tile(mqkv_ref, gate_rows, st_scr, m_scr, False, emit_fwd)

    def conv_out():
        zh = uh[:, CONV_DIM:2 * CONV_DIM] * uh[:, 2 * CONV_DIM:]
        z_prev = jnp.where(i > 0, zh[SUBLANES - 1:SUBLANES, :], 0.0)
        z_next = jnp.where(i < n_tiles - 1, zh[SUBLANES:SUBLANES + 1, :], 0.0)
        z = uc[:, CONV_DIM:2 * CONV_DIM] * uc[:, 2 * CONV_DIM:]
        trow = lax.broadcasted_iota(jnp.int32, (tile, CONV_DIM), 0)
        z_dn = jnp.where(trow == 0, z_prev, pltpu.roll(z, 1, axis=0))
        z_up = jnp.where(trow == tile - 1, z_next, pltpu.roll(z, tile - 1, axis=0))
        conv = convw_ref[0:1, :] * z_dn + convw_ref[1:2, :] * z + convw_ref[2:3, :] * z_up
        return _dot((uc[:, :CONV_DIM] * conv).astype(BF16), wa_ref[...])

    def ga_block(y_a, blk):
        cols = slice(blk * GA_BLOCK, (blk + 1) * GA_BLOCK)
        sg = jax.nn.sigmoid(_dot(h, w_ref[:, C_GA + blk * GA_BLOCK:C_GA + (blk + 1) * GA_BLOCK]))
        pa_ref[:, cols] = (sg * y_a[:, cols]).astype(BF16)

    def advance():
        for _ in range(MLSTM_PIECES_PER_PROJ):
            next(mlstm, None)

    advance()
    y_a = conv_out()
    advance()
    n_ga = D_MODEL // GA_BLOCK
    for blk in range(n_ga):
        ga_block(y_a, blk)
        advance()
    attn_kv()
    for _ in mlstm:
        pass


def _mix_out_kernel(sink_ref, x_ref, hn_ref, pa_ref, aq_ref, kd_ref, kdp_ref, kdn_ref,
                    vt_ref, vtp_ref, vtn_ref, mqkv_ref, smo_ref, grow_ref, hf_ref, bias_ref,
                    mnorm_ref, wg_ref, wb_ref, wc_ref, wo_ref,
                    xo_ref,
                    kwin, vtwin, att_scr, yc_scr, sg_scr, st_scr, m_scr, *, tile):
    i = pl.program_id(1)
    n_tiles = pl.num_programs(1)
    ti = n_tiles - 1 - i
    n_qblk = tile // ATTN_BLOCK

    @pl.when(i == 0)
    def _():
        _init_state(st_scr, m_scr)

    kwin[0:ATTN_BLOCK, :] = kdp_ref[...]
    kwin[ATTN_BLOCK:ATTN_BLOCK + tile, :] = kd_ref[...]
    kwin[ATTN_BLOCK + tile:, :] = kdn_ref[...]
    vtwin[:, 0:ATTN_BLOCK] = vtp_ref[...]
    vtwin[:, ATTN_BLOCK:ATTN_BLOCK + tile] = vt_ref[...]
    vtwin[:, ATTN_BLOCK + tile:] = vtn_ref[...]

    h = hn_ref[...]
    lane = lax.broadcasted_iota(jnp.int32, (ATTN_BLOCK, LANES), 1)
    low = lane < ATTN_HEAD_DIM
    krow = lax.broadcasted_iota(jnp.int32, (3 * ATTN_BLOCK, 2 * ATTN_BLOCK), 0)
    second = lax.broadcasted_iota(jnp.int32, (1, 2 * ATTN_BLOCK), 1) >= ATTN_BLOCK

    gate_cols = 2 * D_MODEL // (n_qblk * ATTN_KV_HEADS)

    def gate_block(blk):
        cols = slice(blk * gate_cols, (blk + 1) * gate_cols)
        sg_scr[:, cols] = jax.nn.sigmoid(_dot(h, wg_ref[:, cols])).astype(BF16)

    def attn_scores(j, g):
        rows = slice(j * ATTN_BLOCK, (j + 1) * ATTN_BLOCK)
        wrows = slice(j * ATTN_BLOCK, (j + 3) * ATTN_BLOCK)
        outside = None
        if j == 0:
            outside = jnp.logical_and(krow < ATTN_BLOCK, ti == 0)
        if j == n_qblk - 1:
            after = jnp.logical_and(krow >= 2 * ATTN_BLOCK, ti == n_tiles - 1)
            outside = after if outside is None else jnp.logical_or(outside, after)
        kw = kwin[wrows, g * LANES:(g + 1) * LANES]
        scores = []
        for pr in range(2):
            pair = g * 2 + pr
            qp = aq_ref[rows, pair * LANES:(pair + 1) * LANES]
            zero = jnp.zeros_like(qp)
            q_both = jnp.concatenate([jnp.where(low, qp, zero), jnp.where(low, zero, qp)], axis=0)
            s = _dot_nt(kw, q_both) + bias_ref[pair]
            if outside is not None:
                s = jnp.where(outside, -jnp.inf, s)
            scores.append(s)
        gate_block(j * ATTN_KV_HEADS + g)
        return scores

    def attn_finish(j, g, scores):
        rows = slice(j * ATTN_BLOCK, (j + 1) * ATTN_BLOCK)
        wrows = slice(j * ATTN_BLOCK, (j + 3) * ATTN_BLOCK)
        vt = vtwin[g * ATTN_HEAD_DIM:(g + 1) * ATTN_HEAD_DIM, wrows]
        for pr, s in enumerate(scores):
            pair = g * 2 + pr
            sk = jnp.where(second, sink_ref[2 * pair + 1] * LOG2E, sink_ref[2 * pair] * LOG2E)
            m = jnp.maximum(jnp.max(s, axis=0, keepdims=True), sk)
            p = jnp.exp2(s - m)
            denom = jnp.sum(p, axis=0, keepdims=True) + jnp.exp2(sk - m)
            o_t = _dot(vt, p.astype(BF16)) / denom
            o_pair = jnp.concatenate([o_t[:, :ATTN_BLOCK], o_t[:, ATTN_BLOCK:]], axis=0)
            att_scr[rows, pair * LANES:(pair + 1) * LANES] = jnp.transpose(o_pair).astype(BF16)

    def emit_bwd(c, hd, out):
        rows = slice(c * MLSTM_CHUNK, (c + 1) * MLSTM_CHUNK)
        sl = slice(hd * MLSTM_HEAD_DIM, (hd + 1) * MLSTM_HEAD_DIM)
        hm = hf_ref[rows, sl] + out
        hm = hm * lax.rsqrt(jnp.mean(hm * hm, axis=-1, keepdims=True) + RMS_EPS)
        yc_scr[rows, sl] = (smo_ref[rows, sl].astype(F32) * (hm * mnorm_ref[:, sl])).astype(BF16)

    gate_rows = [grow_ref[SUBLANES:, c * MLSTM_CHUNK:(c + 1) * MLSTM_CHUNK]
                 for c in range(tile // MLSTM_CHUNK)]
    mlstm = _mlstm_tile(mqkv_ref, gate_rows, st_scr, m_scr, True, emit_bwd)

    units = [(j, g) for j in range(n_qblk) for g in range(ATTN_KV_HEADS)]
    scores = attn_scores(*units[0])
    for k, unit in enumerate(units):
        following = attn_scores(*units[k + 1]) if k + 1 < len(units) else None
        attn_finish(*unit, scores)
        for _ in range(MLSTM_PIECES_PER_ATTN):
            next(mlstm, None)
        scores = following
    next(mlstm, None)
    y_b = _dot(att_scr[...], wb_ref[...]).astype(BF16)
    for _ in mlstm:
        pass

    y_c = _dot(yc_scr[...], wc_ref[...]).astype(BF16)
    merged = pa_ref[...] + sg_scr[:, :D_MODEL] * y_b + sg_scr[:, D_MODEL:] * y_c
    xo_ref[...] = x_ref[...] + _dot(merged, wo_ref[...])


def _mlp_kernel(x_ref, g_ref, gf_ref, wup_ref, wdn_ref, o_ref, *, final_norm):
    x = x_ref[...]
    h = _rmsnorm(x, g_ref[...]).astype(BF16)
    acc = x
    for c in range(D_FF // FF_CHUNK):
        cols = slice(c * FF_CHUNK, (c + 1) * FF_CHUNK)
        up = jnp.maximum(_dot(h, wup_ref[:, cols]), 0.0)
        acc = acc + _dot((up * up).astype(BF16), wdn_ref[cols, :])
    if final_norm:
        acc = _rmsnorm(acc, gf_ref[...])
    o_ref[...] = acc


def _resident(shape):
    nd = len(shape)
    return pl.BlockSpec(shape, lambda b, i: (0,) * nd, pipeline_mode=pl.Buffered(1))


def _layer_resident(shape, layer):
    nd = len(shape)
    return pl.BlockSpec((None,) + shape, lambda b, i: (layer,) + (0,) * nd,
                        pipeline_mode=pl.Buffered(1))


def _params():
    return pltpu.CompilerParams(dimension_semantics=("arbitrary", "arbitrary"),
                                vmem_limit_bytes=VMEM_LIMIT_BYTES)


def _mlstm_state_scratch():
    return [pltpu.VMEM((MLSTM_HEADS, MLSTM_HEAD_DIM, 2 * MLSTM_HEAD_DIM), F32),
            pltpu.VMEM((SUBLANES, LANES), F32)]


def _mix_in(x, pw, layer):
    bsz, seq, _ = x.shape
    tile = TILE
    n_tiles = seq // tile
    halo_per_tile = tile // SUBLANES
    n_halo = seq // SUBLANES

    def tok(width):
        return pl.BlockSpec((None, tile, width), lambda b, i: (b, i, 0))

    in_specs = [
        tok(D_MODEL),
        pl.BlockSpec((None, SUBLANES, D_MODEL),
                     lambda b, i: (b, jnp.maximum(i * halo_per_tile - 1, 0), 0)),
        pl.BlockSpec((None, SUBLANES, D_MODEL),
                     lambda b, i: (b, jnp.minimum((i + 1) * halo_per_tile, n_halo - 1), 0)),
        _layer_resident((1, D_MODEL), layer),
        _layer_resident((D_MODEL, N_PROJ), layer),
        _layer_resident((ATTN_KV_DIM, D_MODEL), layer),
        _layer_resident((3, CONV_DIM), layer),
        _layer_resident((1, LANES), layer),
        _layer_resident((CONV_DIM, D_MODEL), layer),
    ]
    def tok_t(rows):
        return pl.BlockSpec((None, rows, tile), lambda b, i: (b, 0, i))

    def shape(width, dtype=BF16):
        return jax.ShapeDtypeStruct((bsz, seq, width), dtype)

    def shape_t(rows, dtype):
        return jax.ShapeDtypeStruct((bsz, rows, seq), dtype)

    out_shape = [shape(D_MODEL), shape(D_MODEL), shape(ATTN_DIM), shape(2 * LANES),
                 shape_t(ATTN_KV_DIM, BF16), shape(3 * MLSTM_DIM), shape(MLSTM_DIM),
                 shape_t(N_GATES, F32), shape(MLSTM_DIM, F32)]
    out_specs = [tok(D_MODEL), tok(D_MODEL), tok(ATTN_DIM), tok(2 * LANES),
                 tok_t(ATTN_KV_DIM), tok(3 * MLSTM_DIM), tok(MLSTM_DIM),
                 tok_t(N_GATES), tok(MLSTM_DIM)]
    return pl.pallas_call(
        functools.partial(_mix_in_kernel, tile=tile),
        grid=(bsz, n_tiles),
        in_specs=in_specs,
        out_specs=out_specs,
        out_shape=out_shape,
        scratch_shapes=_mlstm_state_scratch(),
        compiler_params=_params(),
        name="mix_in",
    )(x, x, x, pw["g_mix"], pw["w_proj"], pw["w_vt"], pw["conv_w"], pw["gate_b"], pw["w_a"])


def _mix_out(x, mid, pw, layer, attn_bias):
    hn, pa, aq, kd, vt, mqkv, smo, grow, hf = mid
    bsz, seq, _ = x.shape
    tile = TILE
    n_tiles = seq // tile
    blk_per_tile = tile // ATTN_BLOCK
    n_blk = seq // ATTN_BLOCK

    def tok(width):
        return pl.BlockSpec((None, tile, width), lambda b, i: (b, n_tiles - 1 - i, 0))

    def blk_prev(width):
        return pl.BlockSpec(
            (None, ATTN_BLOCK, width),
            lambda b, i: (b, jnp.maximum((n_tiles - 1 - i) * blk_per_tile - 1, 0), 0))

    def blk_next(width):
        return pl.BlockSpec(
            (None, ATTN_BLOCK, width),
            lambda b, i: (b, jnp.minimum((n_tiles - i) * blk_per_tile, n_blk - 1), 0))

    in_specs = [
        pl.BlockSpec(memory_space=pltpu.SMEM),
        tok(D_MODEL), tok(D_MODEL), tok(D_MODEL), tok(ATTN_DIM),
        tok(2 * LANES), blk_prev(2 * LANES), blk_next(2 * LANES),
        pl.BlockSpec((None, ATTN_KV_DIM, tile), lambda b, i: (b, 0, n_tiles - 1 - i)),
        pl.BlockSpec((None, ATTN_KV_DIM, ATTN_BLOCK),
                     lambda b, i: (b, 0, jnp.maximum((n_tiles - 1 - i) * blk_per_tile - 1, 0))),
        pl.BlockSpec((None, ATTN_KV_DIM, ATTN_BLOCK),
                     lambda b, i: (b, 0, jnp.minimum((n_tiles - i) * blk_per_tile, n_blk - 1))),
        tok(3 * MLSTM_DIM), tok(MLSTM_DIM),
        pl.BlockSpec((None, 2 * SUBLANES, tile), lambda b, i: (b, 0, n_tiles - 1 - i)),
        tok(MLSTM_DIM),
        _resident((ATTN_HEADS // 2, 3 * ATTN_BLOCK, 2 * ATTN_BLOCK)),
        _layer_resident((1, MLSTM_DIM), layer),
        _layer_resident((D_MODEL, 2 * D_MODEL), layer),
        _layer_resident((ATTN_DIM, D_MODEL), layer),
        _layer_resident((MLSTM_DIM, D_MODEL), layer),
        _layer_resident((D_MODEL, D_MODEL), layer),
    ]
    scratch = [pltpu.VMEM((tile + 2 * ATTN_BLOCK, 2 * LANES), BF16),
               pltpu.VMEM((ATTN_KV_DIM, tile + 2 * ATTN_BLOCK), BF16),
               pltpu.VMEM((tile, ATTN_DIM), BF16),
               pltpu.VMEM((tile, MLSTM_DIM), BF16),
               pltpu.VMEM((tile, 2 * D_MODEL), BF16)]
    return pl.pallas_call(
        functools.partial(_mix_out_kernel, tile=tile),
        grid=(bsz, n_tiles),
        in_specs=in_specs,
        out_specs=tok(D_MODEL),
        out_shape=jax.ShapeDtypeStruct((bsz, seq, D_MODEL), F32),
        scratch_shapes=scratch + _mlstm_state_scratch(),
        compiler_params=_params(),
        name="mix_out",
    )(pw["sink"][layer], x, hn, pa, aq, kd, kd, kd, vt, vt, vt, mqkv, smo, grow, hf,
      attn_bias, pw["mnorm_g"], pw["w_gbc"], pw["w_b"], pw["w_c"], pw["w_o"])


def _mlp(x, pw, layer, final_norm):
    bsz, seq, _ = x.shape
    tile = MLP_TILE
    tok = pl.BlockSpec((None, tile, D_MODEL), lambda b, i: (b, i, 0))
    return pl.pallas_call(
        functools.partial(_mlp_kernel, final_norm=final_norm),
        grid=(bsz, seq // tile),
        in_specs=[tok, _layer_resident((1, D_MODEL), layer), _resident((1, D_MODEL)),
                  _layer_resident((D_MODEL, D_FF), layer), _layer_resident((D_FF, D_MODEL), layer)],
        out_specs=tok,
        out_shape=jax.ShapeDtypeStruct((bsz, seq, D_MODEL), F32),
        compiler_params=_params(),
        name="mlp",
    )(x, pw["g_mlp"], pw["g_final"], pw["w_up"], pw["w_down"])


def _attention_bias():
    q = jnp.arange(ATTN_BLOCK)[None, :]
    k = jnp.arange(3 * ATTN_BLOCK)[:, None]
    dist = jnp.abs(k - ATTN_BLOCK - q).astype(F32)
    slopes = jnp.exp2(-8.0 * (jnp.arange(ATTN_HEADS, dtype=F32) + 1.0) / ATTN_HEADS)
    bias = jnp.where(dist <= WINDOW, -slopes[:, None, None] * dist * LOG2E, -jnp.inf)
    bias = bias.reshape(ATTN_HEADS // 2, 2, 3 * ATTN_BLOCK, ATTN_BLOCK)
    return jnp.concatenate([bias[:, 0], bias[:, 1]], axis=2)


def _split_kernel(a_ref, b_ref, o_ref, *, first_shifted, end_shifted, gate_block):
    j = pl.program_id(1)
    a = a_ref[...]
    shifted = jnp.concatenate([a[N_GATES:], b_ref[...]], axis=0)
    src = jnp.where(jnp.logical_and(j >= first_shifted, j < end_shifted), shifted, a)
    if gate_block is not None:
        row = lax.broadcasted_iota(jnp.int32, src.shape, 0)
        src = jnp.where(jnp.logical_and(j == gate_block, row >= N_GATES), 0.0, src)
    o_ref[...] = jnp.transpose(src).astype(BF16)


def _split_input_projection(w_in):
    depth, d_model, n_in = w_in.shape
    wt = jnp.swapaxes(w_in, 1, 2)
    blk = SPLIT_COLS
    assert n_in % N_GATES == 0
    ga_blk = C_GA // blk
    gate_blk = C_GATE // blk
    gbc_blk = (C_GA + D_MODEL) // blk
    assert C_GA % blk == 0 and C_GATE % blk == 0 and N_PROJ == C_GATE + blk

    def call(n_blocks, a_index, kernel_kwargs, name):
        return pl.pallas_call(
            functools.partial(_split_kernel, **kernel_kwargs),
            grid=(depth, n_blocks),
            in_specs=[pl.BlockSpec((None, blk, d_model), lambda l, j: (l, a_index(j), 0)),
                      pl.BlockSpec((None, N_GATES, d_model),
                                   lambda l, j: (l, jnp.minimum((a_index(j) + 1) * (blk // N_GATES),
                                                                n_in // N_GATES - 1), 0))],
            out_specs=pl.BlockSpec((None, d_model, blk), lambda l, j: (l, 0, j)),
            out_shape=jax.ShapeDtypeStruct((depth, d_model, n_blocks * blk), BF16),
            compiler_params=_params(),
            name=name,
        )(wt, wt)

    w_proj = call(N_PROJ // blk, lambda j: jnp.where(j == gate_blk, ga_blk, j),
                  dict(first_shifted=ga_blk, end_shifted=gate_blk, gate_block=gate_blk), "split_proj")
    w_gbc = call(2 * D_MODEL // blk, lambda j: j + gbc_blk,
                 dict(first_shifted=0, end_shifted=2 * D_MODEL // blk, gate_block=None), "split_gbc")
    return w_proj, w_gbc


def _prepare_params(w_in, conv_w, attn_sink, mlstm_gate_b, mlstm_norm_g, w_out_a, w_out_b, w_out_c,
                    w_o, norm_mix_g, norm_mlp_g, w_mlp_up, w_mlp_down, norm_final_g):
    depth = w_in.shape[0]
    w_proj, w_gbc = _split_input_projection(w_in)
    gate_b = jnp.concatenate([mlstm_gate_b, jnp.zeros((depth, LANES - N_GATES), F32)], axis=1)
    return {
        "w_proj": w_proj,
        "w_gbc": w_gbc,
        "w_vt": jnp.swapaxes(w_in[:, :, C_AV:C_AV + ATTN_KV_DIM], 1, 2).astype(BF16),
        "conv_w": conv_w,
        "sink": attn_sink,
        "gate_b": gate_b[:, None, :],
        "mnorm_g": mlstm_norm_g[:, None, :],
        "w_a": w_out_a.astype(BF16),
        "w_b": w_out_b.astype(BF16),
        "w_c": w_out_c.astype(BF16),
        "w_o": w_o.astype(BF16),
        "g_mix": norm_mix_g[:, None, :],
        "g_mlp": norm_mlp_g[:, None, :],
        "w_up": w_mlp_up.astype(BF16),
        "w_down": w_mlp_down.astype(BF16),
        "g_final": norm_final_g[None, :],
    }


def kernel(x_prompt, x_sample, w_in, conv_w, attn_sink, mlstm_gate_b, mlstm_norm_g, w_out_a,
           w_out_b, w_out_c, w_o, norm_mix_g, norm_mlp_g, w_mlp_up, w_mlp_down, norm_final_g):
    depth = w_in.shape[0]
    pw = _prepare_params(w_in, conv_w, attn_sink, mlstm_gate_b, mlstm_norm_g, w_out_a, w_out_b,
                         w_out_c, w_o, norm_mix_g, norm_mlp_g, w_mlp_up, w_mlp_down, norm_final_g)
    attn_bias = _attention_bias()

    def trunk(x):
        for layer in range(depth):
            x = _mix_out(x, _mix_in(x, pw, layer), pw, layer, attn_bias)
            x = _mlp(x, pw, layer, final_norm=(layer == depth - 1))
        return x

    return (trunk(x_prompt), trunk(x_sample))
```
